```python
import functools
import numpy as np
import jax
import jax.numpy as jnp
from jax import lax

D_MODEL = 1024
BATCH = 2
SEQ = 8192
DEPTH = 4
DEC_BATCH = 128
DEC_SEQ = 4
PAST_LEN = 8192
PAGE_SIZE = 128

F32 = jnp.float32
EPS = 1e-6

MIX_WIDTH = D_MODEL
N_GROUPS = 4
GROUP_W = MIX_WIDTH // N_GROUPS

POOL_WINDOWS = (2, 4, 8, 16)
POOL_GROUP_C = GROUP_W // len(POOL_WINDOWS)
POOL_PAD = max(POOL_WINDOWS) - 1

MLA_HEADS = 4
MLA_NOPE = 64
MLA_ROPE = 32
MLA_V = GROUP_W // MLA_HEADS
MLA_Q_LORA = 256
MLA_KV_LORA = 128
MLA_SCALE = (MLA_NOPE + MLA_ROPE) ** -0.5
ROPE_THETA = 10000.0
Q_BLOCK = 128

HG_HEADS = 4
HG_DK = 128
HG_DV = GROUP_W // HG_HEADS
HG_CHUNK = 64

CONV_W = 31
CONV_PAD = CONV_W - 1
CONV_C = GROUP_W

N_MEM = 256
CA_HEADS = 4
CA_HD = 128
CA_W = CA_HEADS * CA_HD

D_FF = 2816
N_EXPERTS = 8
TOP_K = 2
MOE_BLOCK = 256
N_DENSE = (DEPTH + 1) // 2
N_MOE = DEPTH // 2

IN_SIZES = (GROUP_W,
            MLA_Q_LORA, MLA_KV_LORA, MLA_ROPE,
            HG_HEADS * HG_DK, HG_HEADS * HG_DK, GROUP_W, GROUP_W,
            2 * CONV_C)
IN_COLS = sum(IN_SIZES)
IN_SPLITS = tuple(int(s) for s in np.cumsum(IN_SIZES)[:-1])

kernel_name = 'hymba_pool_mla_hgrn2_conformer_step'


def rms_norm(x, w):
    xf = x.astype(F32)
    y = xf * lax.rsqrt(jnp.mean(xf * xf, axis=-1, keepdims=True) + EPS)
    return (y * w.astype(F32)).astype(x.dtype)


def layer_norm(x, w, b):
    xf = x.astype(F32)
    xc = xf - jnp.mean(xf, axis=-1, keepdims=True)
    var = jnp.mean(xc * xc, axis=-1, keepdims=True)
    return (xc * lax.rsqrt(var + EPS) * w.astype(F32) + b.astype(F32)).astype(x.dtype)


def rope(x, pos):
    half = x.shape[-1] // 2
    inv = 1.0 / (ROPE_THETA ** (jnp.arange(half, dtype=F32) / half))
    ang = pos.astype(F32)[:, None] * inv[None, :]
    cos = jnp.cos(ang)[None, :, None, :]
    sin = jnp.sin(ang)[None, :, None, :]
    xf = x.astype(F32)
    x1, x2 = xf[..., :half], xf[..., half:]
    return jnp.concatenate([x1 * cos - x2 * sin, x2 * cos + x1 * sin], axis=-1).astype(x.dtype)


def swiglu(x, wg, wu, wd):
    return (jax.nn.silu(x @ wg) * (x @ wu)) @ wd


def pool_mixer(u, prefix, pos, w_pool, scale):
    n, L, c = u.shape
    ext = jnp.concatenate([prefix.astype(u.dtype), u], axis=1)
    cs = jnp.concatenate([jnp.zeros((n, 1, c), F32), jnp.cumsum(ext.astype(F32), axis=1)], axis=1)
    hi = cs[:, POOL_PAD + 1:]
    pooled = []
    for g, w in enumerate(POOL_WINDOWS):
        sl = slice(g * POOL_GROUP_C, (g + 1) * POOL_GROUP_C)
        lo = cs[:, POOL_PAD + 1 - w: POOL_PAD + 1 - w + L, sl]
        cnt = jnp.minimum(w, pos + 1).astype(F32)
        pooled.append((hi[..., sl] - lo) / cnt[None, :, None])
    d = (jnp.concatenate(pooled, axis=-1) - u.astype(F32)).reshape(n, L, len(POOL_WINDOWS), POOL_GROUP_C)
    y = jnp.einsum('nlgc,gcd->nlgd', d, w_pool.astype(F32)).reshape(n, L, c) * scale.astype(F32)
    return y.astype(u.dtype), ext[:, -POOL_PAD:]


def mla_project(c_q, c_kv_raw, k_r_raw, pos, lw):
    n, L, _ = c_q.shape
    q = (rms_norm(c_q, lw['mla_q_norm_w']) @ lw['mla_w_uq']).reshape(n, L, MLA_HEADS, MLA_NOPE + MLA_ROPE)
    q_nope = rms_norm(q[..., :MLA_NOPE], lw['mla_qn_nope_w'])
    q_rope = rope(rms_norm(q[..., MLA_NOPE:], lw['mla_qn_rope_w']), pos)
    q = jnp.concatenate([q_nope, q_rope], axis=-1)
    c_kv = rms_norm(c_kv_raw, lw['mla_kv_norm_w'])
    k_rope = rope(rms_norm(k_r_raw, lw['mla_kn_rope_w'])[:, :, None, :], pos)[:, :, 0, :]
    return q, c_kv, k_rope


def mla_expand(c_kv, k_rope, lw):
    kv = (c_kv @ lw['mla_w_ukv']).reshape(c_kv.shape[:-1] + (MLA_HEADS, MLA_NOPE + MLA_V))
    k_nope = rms_norm(kv[..., :MLA_NOPE], lw['mla_kn_nope_w'])
    v = kv[..., MLA_NOPE:]
    k_r = jnp.broadcast_to(k_rope[..., None, :], k_nope.shape[:-1] + (MLA_ROPE,)).astype(k_nope.dtype)
    return jnp.concatenate([k_nope, k_r], axis=-1), v


def mla_attend_prompt(q, c_kv, k_rope, lw):
    n, L, h, dq = q.shape
    k, v = mla_expand(c_kv, k_rope, lw)
    qb_len = min(Q_BLOCK, L)
    nb = L // qb_len
    qb = q.reshape(n, nb, qb_len, h, dq).swapaxes(0, 1)
    key_pos = jnp.arange(L)

    def block(args):
        qi, start = args
        s = jnp.einsum('nqhd,nkhd->nhqk', qi, k, preferred_element_type=F32) * MLA_SCALE
        qpos = start + jnp.arange(qb_len)
        s = jnp.where(key_pos[None, :] <= qpos[:, None], s, -jnp.inf)
        p = jax.nn.softmax(s, axis=-1).astype(v.dtype)
        return jnp.einsum('nhqk,nkhd->nqhd', p, v)

    o = lax.map(block, (qb, jnp.arange(nb) * qb_len))
    return o.swapaxes(0, 1).reshape(n, L, h * MLA_V)


def mla_attend_sample(q, c_kv, k_rope, lw, cache_c, cache_r, page_table):
    nb, s_len, h, _ = q.shape
    n_past = page_table.shape[1] * PAGE_SIZE
    mask = jnp.concatenate([jnp.ones((s_len, n_past), dtype=bool),
                            jnp.tril(jnp.ones((s_len, s_len), dtype=bool))], axis=1)

    def one(args):
        pt, qi, c_new, r_new = args
        c = jnp.concatenate([cache_c[pt].reshape(n_past, MLA_KV_LORA), c_new.astype(cache_c.dtype)], axis=0)
        r = jnp.concatenate([cache_r[pt].reshape(n_past, MLA_ROPE), r_new.astype(cache_r.dtype)], axis=0)
        k, v = mla_expand(c, r, lw)
        s = jnp.einsum('qhd,khd->hqk', qi, k, preferred_element_type=F32) * MLA_SCALE
        s = jnp.where(mask[None], s, -jnp.inf)
        p = jax.nn.softmax(s, axis=-1).astype(v.dtype)
        return jnp.einsum('hqk,khd->qhd', p, v)

    o = lax.map(one, (page_table, q, c_kv, k_rope))
    return o.reshape(nb, s_len, h * MLA_V).astype(q.dtype)


def hgrn2_chunked(q, log_f, k, v, s0):
    n, L, h, dk = q.shape
    dv = v.shape[-1]
    c = HG_CHUNK if L % HG_CHUNK == 0 else L
    nc = L // c

    def chunks(a):
        return a.reshape((n, nc, c) + a.shape[2:]).swapaxes(0, 1)

    causal = jnp.tril(jnp.ones((c, c), dtype=bool))[None, :, :, None, None]

    def step(s, inp):
        qc, lfc, kc, vc = inp
        b = jnp.cumsum(lfc, axis=1)
        o_inter = jnp.einsum('nthk,nhkv->nthv', qc * jnp.exp(b), s)
        decay = jnp.exp(jnp.where(causal, b[:, :, None] - b[:, None, :], -jnp.inf))
        a = jnp.einsum('nthk,ntshk,nshk->nhts', qc, decay, kc)
        o_intra = jnp.einsum('nhts,nshv->nthv', a, vc)
        b_last = b[:, -1]
        s_new = jnp.exp(b_last)[..., None] * s + jnp.einsum('nshk,nshv->nhkv', kc * jnp.exp(b_last[:, None] - b), vc)
        return s_new, o_inter + o_intra

    s_last, o = lax.scan(step, s0, (chunks(q), chunks(log_f), chunks(k), chunks(v)))
    return s_last, o.swapaxes(0, 1).reshape(n, L, h, dv)


def hgrn2_mixer(q_in, f_in, i_in, g_in, s0, lb, onorm_w):
    n, L, _ = q_in.shape
    q = q_in.astype(F32).reshape(n, L, HG_HEADS, HG_DK)
    zf = f_in.astype(F32).reshape(n, L, HG_HEADS, HG_DK)
    lbh = lb.reshape(HG_HEADS, HG_DK)
    log_f = jnp.log(lbh + (1.0 - lbh) * jax.nn.sigmoid(zf))
    k = (1.0 - lbh) * jax.nn.sigmoid(-zf)
    v = i_in.astype(F32).reshape(n, L, HG_HEADS, HG_DV)
    s_last, o = hgrn2_chunked(q, log_f, k, v, s0.astype(F32))
    o = rms_norm(o, onorm_w) * jax.nn.silu(g_in.astype(F32).reshape(n, L, HG_HEADS, HG_DV))
    return o.reshape(n, L, GROUP_W).astype(q_in.dtype), s_last.astype(s0.dtype)


def conv_module(u, prefix, lw):
    a, b = jnp.split(u, 2, axis=-1)
    g = a * jax.nn.sigmoid(b)
    ext = jnp.concatenate([prefix.astype(g.dtype), g], axis=1)
    y = lax.conv_general_dilated(ext, lw['conv_w'][:, None, :].astype(ext.dtype), window_strides=(1,),
                                 padding='VALID', dimension_numbers=('NWC', 'WIO', 'NWC'),
                                 feature_group_count=CONV_C)
    y = layer_norm(y + lw['conv_b'], lw['conv_ln_w'], lw['conv_ln_b'])
    y = jax.nn.silu(y) @ lw['conv_pw_w'] + lw['conv_pw_b']
    return y, ext[:, -CONV_PAD:]


def memory_kv(mem, lw):
    n, m, _ = mem.shape
    mm = rms_norm(mem, lw['mem_norm_w'])
    k = rms_norm((mm @ lw['ca_wk']).reshape(n, m, CA_HEADS, CA_HD), lw['ca_kn_w'])
    v = (mm @ lw['ca_wv']).reshape(n, m, CA_HEADS, CA_HD)
    return k, v


def cross_attend(h, mem_k, mem_v, lw):
    n, L, _ = h.shape
    q = rms_norm((h @ lw['ca_wq']).reshape(n, L, CA_HEADS, CA_HD), lw['ca_qn_w'])
    s = jnp.einsum('nlhd,nmhd->nhlm', q, mem_k.astype(q.dtype), preferred_element_type=F32) * (CA_HD ** -0.5)
    p = jax.nn.softmax(s, axis=-1).astype(q.dtype)
    o = jnp.einsum('nhlm,nmhd->nlhd', p, mem_v.astype(q.dtype)).reshape(n, L, CA_W)
    return o @ lw['ca_wo']


def moe_ffn(h, router, wg, wu, wd):
    n, L, d = h.shape
    t = h.reshape(n * L, d)
    T = n * L
    logits = jnp.einsum('td,de->te', t, router, preferred_element_type=F32)
    top_l, top_e = lax.top_k(logits, TOP_K)
    gates = jax.nn.softmax(top_l, axis=-1)
    e_flat = top_e.reshape(-1)
    tok = jnp.repeat(jnp.arange(T), TOP_K)
    g_flat = gates.reshape(-1)
    order = jnp.argsort(e_flat)
    e_s, tok_s, g_s = e_flat[order], tok[order], g_flat[order]
    counts = jnp.zeros((N_EXPERTS,), jnp.int32).at[e_flat].add(1)
    padded = (counts + MOE_BLOCK - 1) // MOE_BLOCK * MOE_BLOCK
    pad_end = jnp.cumsum(padded)
    pad_start = pad_end - padded
    raw_start = jnp.cumsum(counts) - counts
    slot = pad_start[e_s] + jnp.arange(T * TOP_K) - raw_start[e_s]
    n_blk = -(-(T * TOP_K + N_EXPERTS * (MOE_BLOCK - 1)) // MOE_BLOCK)
    P = n_blk * MOE_BLOCK
    buf = jnp.zeros((P, d), t.dtype).at[slot].set(t[tok_s])
    blk_e = jnp.minimum(jnp.searchsorted(pad_end, jnp.arange(n_blk) * MOE_BLOCK, side='right'), N_EXPERTS - 1)

    def run(args):
        xb, e = args
        return swiglu(xb, wg[e], wu[e], wd[e])

    y_buf = lax.map(run, (buf.reshape(n_blk, MOE_BLOCK, d), blk_e)).reshape(P, d)
    y = jnp.zeros((T, d), F32).at[tok_s].add(y_buf[slot].astype(F32) * g_s[:, None])
    return y.reshape(n, L, d).astype(h.dtype)


def trunk_layer(x, start, pool_prefix, conv_prefix, hg_s0, mem_k, mem_v, attend, ffn, lb, lw):
    n, L, _ = x.shape
    pos = start + jnp.arange(L, dtype=jnp.int32)
    h = rms_norm(x, lw['norm1_w'])
    z = h @ lw['w_in']
    u_a, c_q, c_kv_raw, k_r_raw, g_q, g_f, g_i, g_o, u_d = jnp.split(z, IN_SPLITS, axis=-1)
    o_a, pool_state = pool_mixer(u_a, pool_prefix, pos, lw['pool_w'], lw['pool_scale'])
    q, c_kv, k_rope = mla_project(c_q, c_kv_raw, k_r_raw, pos, lw)
    o_b = attend(q, c_kv, k_rope, lw)
    o_c, hg_state = hgrn2_mixer(g_q, g_f, g_i, g_o, hg_s0, lb, lw['hg_onorm_w'])
    o_d, conv_state = conv_module(u_d, conv_prefix, lw)
    o = jnp.stack([o_a, o_b.astype(o_a.dtype), o_c, o_d.astype(o_a.dtype)], axis=2)
    o = rms_norm(o, lw['grp_norm_w'].reshape(N_GROUPS, GROUP_W)).reshape(n, L, MIX_WIDTH)
    x = x + o @ lw['w_out']
    x = x + cross_attend(rms_norm(x, lw['norm2_w']), mem_k, mem_v, lw)
    x = x + ffn(rms_norm(x, lw['norm3_w']))
    return x, pool_state, c_kv, k_rope, hg_state, conv_state


def setup_inputs(seed: int = 0) -> dict:
    key = jax.random.key(seed)
    ks = iter(jax.random.split(key, 80))

    def nrm(shape, scale=1.0):
        return jax.random.normal(next(ks), shape, F32) * scale

    def gain(shape):
        return 1.0 + nrm(shape, 0.02)

    n_pages = PAST_LEN // PAGE_SIZE
    n_used = DEC_BATCH * n_pages
    n_phys = n_used + n_used // 4
    page_table = jax.random.permutation(next(ks), n_phys)[:n_used].reshape(DEC_BATCH, n_pages).astype(jnp.int32)
    D = D_MODEL
    return {
        'x_prompt': nrm((BATCH, SEQ, D)),
        'x_sample': nrm((DEC_BATCH, DEC_SEQ, D)),
        'cache_kv_latent': nrm((DEPTH, n_phys, PAGE_SIZE, MLA_KV_LORA)),
        'cache_k_rope': nrm((DEPTH, n_phys, PAGE_SIZE, MLA_ROPE)),
        'cache_mem_k': nrm((DEPTH, DEC_BATCH, N_MEM, CA_HEADS, CA_HD)),
        'cache_mem_v': nrm((DEPTH, DEC_BATCH, N_MEM, CA_HEADS, CA_HD)),
        'state_pool': nrm((DEPTH, DEC_BATCH, POOL_PAD, GROUP_W)),
        'state_hgrn': nrm((DEPTH, DEC_BATCH, HG_HEADS, HG_DK, HG_DV), 0.3),
        'state_conv': nrm((DEPTH, DEC_BATCH, CONV_PAD, CONV_C), 0.5),
        'page_table': page_table,
        'mem_prompt': nrm((BATCH, N_MEM, D)),
        'norm1_w': gain((DEPTH, D)),
        'w_in': nrm((DEPTH, D, IN_COLS), D ** -0.5),
        'pool_w': nrm((DEPTH, len(POOL_WINDOWS), POOL_GROUP_C, POOL_GROUP_C), POOL_GROUP_C ** -0.5),
        'pool_scale': gain((DEPTH, GROUP_W)),
        'mla_q_norm_w': gain((DEPTH, MLA_Q_LORA)),
        'mla_w_uq': nrm((DEPTH, MLA_Q_LORA, MLA_HEADS * (MLA_NOPE + MLA_ROPE)), MLA_Q_LORA ** -0.5),
        'mla_kv_norm_w': gain((DEPTH, MLA_KV_LORA)),
        'mla_w_ukv': nrm((DEPTH, MLA_KV_LORA, MLA_HEADS * (MLA_NOPE + MLA_V)), MLA_KV_LORA ** -0.5),
        'mla_qn_nope_w': gain((DEPTH, MLA_NOPE)),
        'mla_qn_rope_w': gain((DEPTH, MLA_ROPE)),
        'mla_kn_nope_w': gain((DEPTH, MLA_NOPE)),
        'mla_kn_rope_w': gain((DEPTH, MLA_ROPE)),
        'hg_lb_param': nrm((DEPTH, HG_HEADS * HG_DK), 0.1),
        'hg_onorm_w': gain((DEPTH, HG_DV)),
        'conv_w': nrm((DEPTH, CONV_W, CONV_C), CONV_W ** -0.5),
        'conv_b': nrm((DEPTH, CONV_C), 0.02),
        'conv_ln_w': gain((DEPTH, CONV_C)),
        'conv_ln_b': nrm((DEPTH, CONV_C), 0.02),
        'conv_pw_w': nrm((DEPTH, CONV_C, CONV_C), CONV_C ** -0.5),
        'conv_pw_b': nrm((DEPTH, CONV_C), 0.02),
        'grp_norm_w': gain((DEPTH, MIX_WIDTH)),
        'w_out': nrm((DEPTH, MIX_WIDTH, D), MIX_WIDTH ** -0.5),
        'norm2_w': gain((DEPTH, D)),
        'mem_norm_w': gain((DEPTH, D)),
        'ca_wq': nrm((DEPTH, D, CA_W), D ** -0.5),
        'ca_wk': nrm((DEPTH, D, CA_W), D ** -0.5),
        'ca_wv': nrm((DEPTH, D, CA_W), D ** -0.5),
        'ca_qn_w': gain((DEPTH, CA_HD)),
        'ca_kn_w': gain((DEPTH, CA_HD)),
        'ca_wo': nrm((DEPTH, CA_W, D), CA_W ** -0.5),
        'norm3_w': gain((DEPTH, D)),
        'ffn_w_gate': nrm((N_DENSE, D, D_FF), D ** -0.5),
        'ffn_w_up': nrm((N_DENSE, D, D_FF), D ** -0.5),
        'ffn_w_down': nrm((N_DENSE, D_FF, D), D_FF ** -0.5),
        'moe_router': nrm((N_MOE, D, N_EXPERTS), D ** -0.5),
        'moe_w_gate': nrm((N_MOE, N_EXPERTS, D, D_FF), D ** -0.5),
        'moe_w_up': nrm((N_MOE, N_EXPERTS, D, D_FF), D ** -0.5),
        'moe_w_down': nrm((N_MOE, N_EXPERTS, D_FF, D), D_FF ** -0.5),
    }


def reference(x_prompt, x_sample, cache_kv_latent, cache_k_rope, cache_mem_k, cache_mem_v, state_pool,
              state_hgrn, state_conv, page_table, mem_prompt, norm1_w, w_in, pool_w, pool_scale,
              mla_q_norm_w, mla_w_uq, mla_kv_norm_w, mla_w_ukv, mla_qn_nope_w, mla_qn_rope_w,
              mla_kn_nope_w, mla_kn_rope_w, hg_lb_param, hg_onorm_w, conv_w, conv_b, conv_ln_w, conv_ln_b,
              conv_pw_w, conv_pw_b, grp_norm_w, w_out, norm2_w, mem_norm_w, ca_wq, ca_wk, ca_wv, ca_qn_w,
              ca_kn_w, ca_wo, norm3_w, ffn_w_gate, ffn_w_up, ffn_w_down, moe_router, moe_w_gate, moe_w_up,
              moe_w_down):
    sm = jax.nn.softmax(hg_lb_param.astype(F32), axis=0)
    lower_bounds = jnp.cumsum(sm, axis=0) - sm[:1]
    n_past = page_table.shape[1] * PAGE_SIZE
    bp = x_prompt.shape[0]
    xp, xs = x_prompt, x_sample
    pool_zero = jnp.zeros((bp, POOL_PAD, GROUP_W), x_prompt.dtype)
    conv_zero = jnp.zeros((bp, CONV_PAD, CONV_C), x_prompt.dtype)
    hg_zero = jnp.zeros((bp, HG_HEADS, HG_DK, HG_DV), x_prompt.dtype)
    pp, pkv, pkr, phg, pcv, pmk, pmv = [], [], [], [], [], [], []
    sp, skv, skr, shg, scv = [], [], [], [], []
    for l in range(DEPTH):
        lw = dict(norm1_w=norm1_w[l], w_in=w_in[l], pool_w=pool_w[l], pool_scale=pool_scale[l],
                  mla_q_norm_w=mla_q_norm_w[l], mla_w_uq=mla_w_uq[l], mla_kv_norm_w=mla_kv_norm_w[l],
                  mla_w_ukv=mla_w_ukv[l], mla_qn_nope_w=mla_qn_nope_w[l], mla_qn_rope_w=mla_qn_rope_w[l],
                  mla_kn_nope_w=mla_kn_nope_w[l], mla_kn_rope_w=mla_kn_rope_w[l], hg_onorm_w=hg_onorm_w[l],
                  conv_w=conv_w[l], conv_b=conv_b[l], conv_ln_w=conv_ln_w[l], conv_ln_b=conv_ln_b[l],
                  conv_pw_w=conv_pw_w[l], conv_pw_b=conv_pw_b[l], grp_norm_w=grp_norm_w[l], w_out=w_out[l],
                  norm2_w=norm2_w[l], mem_norm_w=mem_norm_w[l], ca_wq=ca_wq[l], ca_wk=ca_wk[l], ca_wv=ca_wv[l],
                  ca_qn_w=ca_qn_w[l], ca_kn_w=ca_kn_w[l], ca_wo=ca_wo[l], norm3_w=norm3_w[l])
        j = l // 2
        if l % 2 == 0:
            ffn = functools.partial(swiglu, wg=ffn_w_gate[j], wu=ffn_w_up[j], wd=ffn_w_down[j])
        else:
            ffn = functools.partial(moe_ffn, router=moe_router[j], wg=moe_w_gate[j], wu=moe_w_up[j],
                                    wd=moe_w_down[j])
        mk_p, mv_p = memory_kv(mem_prompt, lw)
        xp, a1, a2, a3, a4, a5 = trunk_layer(xp, 0, pool_zero, conv_zero, hg_zero, mk_p, mv_p,
                                             mla_attend_prompt, ffn, lower_bounds[l], lw)
        pp.append(a1); pkv.append(a2); pkr.append(a3); phg.append(a4); pcv.append(a5)
        pmk.append(mk_p); pmv.append(mv_p)
        attend_s = functools.partial(mla_attend_sample, cache_c=cache_kv_latent[l], cache_r=cache_k_rope[l],
                                     page_table=page_table)
        xs, b1, b2, b3, b4, b5 = trunk_layer(xs, n_past, state_pool[l], state_conv[l], state_hgrn[l],
                                             cache_mem_k[l], cache_mem_v[l], attend_s, ffn, lower_bounds[l], lw)
        sp.append(b1); skv.append(b2); skr.append(b3); shg.append(b4); scv.append(b5)
    return (xp, xs, jnp.stack(pp), jnp.stack(pkv), jnp.stack(pkr), jnp.stack(phg), jnp.stack(pcv),
            jnp.stack(pmk), jnp.stack(pmv), jnp.stack(sp), jnp.stack(skv), jnp.stack(skr), jnp.stack(shg),
            jnp.stack(scv))
```

```python
import functools

import numpy as np
import jax
import jax.numpy as jnp
from jax import lax
from jax.experimental import pallas as pl
from jax.experimental.pallas import tpu as pltpu

F32 = jnp.float32
BF16 = jnp.bfloat16
EPS = 1e-6

D_MODEL = 1024
GROUP_W = 256
POOL_WINDOWS = (2, 4, 8, 16)
POOL_PAD = 15
MLA_HEADS = 4
MLA_NOPE = 64
MLA_ROPE = 32
MLA_V = 64
MLA_Q_LORA = 256
MLA_KV_LORA = 128
MLA_SCALE = (MLA_NOPE + MLA_ROPE) ** -0.5
ROPE_THETA = 10000.0
PAGE_SIZE = 128
HG_HEADS = 4
HG_DK = 128
HG_DV = 64
CONV_W = 31
CONV_PAD = 30
CONV_C = 256
N_MEM = 256
CA_HEADS = 4
CA_HD = 128
CA_W = 512
D_FF = 2816
N_EXPERTS = 8

LANES = 128
VMEM_LIMIT_BYTES = 56 * 1024 * 1024
TOK_TILE = 512
FF_CHUNK = 256
MOE_TILE = 512
HG_CHUNK = 64
HG_SUB = 16
PAGES_PER_STEP = 16


def _cparams(*sem):
    return pltpu.CompilerParams(dimension_semantics=sem, vmem_limit_bytes=VMEM_LIMIT_BYTES)


def _rms(x, w, n=None):
    n = x.shape[-1] if n is None else n
    r = lax.rsqrt(jnp.sum(x * x, axis=-1, keepdims=True) * (1.0 / n) + EPS)
    return x * r * w


def _dot(a, b):
    return jnp.dot(a, b, preferred_element_type=F32)


def _dot_nt(a, b):
    return lax.dot_general(a, b, (((1,), (1,)), ((), ())), preferred_element_type=F32)


def _lane(shape):
    return lax.broadcasted_iota(jnp.int32, shape, len(shape) - 1)


def _row(shape):
    return lax.broadcasted_iota(jnp.int32, shape, len(shape) - 2)


def _split(a):
    hi = a.astype(BF16)
    return hi, (a - hi.astype(F32)).astype(BF16)


def _act(a, hp):
    return _split(a) if hp else (a.astype(BF16),)


def _wts(refs, idx=None):
    return tuple(r[...] if idx is None else r[idx] for r in refs)


def _mm(a, b, nt=False):
    d = _dot_nt if nt else _dot
    out = d(a[0], b[0])
    if len(a) > 1:
        out = out + d(a[1], b[0]) + d(a[0], b[1])
    return out


def _w(w, hp):
    return list(_split(w)) if hp else [w.astype(BF16)]


def _full(a, nargs):
    zeros = (0,) * a.ndim
    if nargs == 1:
        return pl.BlockSpec(a.shape, lambda i: zeros)
    if nargs == 2:
        return pl.BlockSpec(a.shape, lambda i, j: zeros)
    return pl.BlockSpec(a.shape, lambda i, j, k: zeros)


def _norm_matmul_kernel(x_ref, nw_ref, *rest, splits, slot_norm, nw):
    w_refs, rest = rest[:nw], rest[nw:]
    if slot_norm:
        sw_ref, out_refs = rest[0], rest[1:]
    else:
        out_refs = rest
    h = _act(_rms(x_ref[...], nw_ref[...]), nw == 2)
    off = 0
    for idx, (o_ref, n) in enumerate(zip(out_refs, splits)):
        y = _mm(h, _wts(w_refs, (slice(None), slice(off, off + n))))
        if slot_norm and idx == 0:
            y = jnp.concatenate(
                [_rms(y[:, s:s + LANES], sw_ref[...]) for s in range(0, n, LANES)], axis=1)
        o_ref[...] = y
        off += n


def norm_matmul(x, nw, w, splits, slot_norm_w=None, tile=TOK_TILE, name="norm_matmul"):
    t, d = x.shape
    n = w[0].shape[1]
    assert sum(splits) == n and t % tile == 0
    in_specs = [pl.BlockSpec((tile, d), lambda i: (i, 0)), pl.BlockSpec((1, d), lambda i: (0, 0))]
    in_specs += [_full(a, 1) for a in w]
    args = [x, nw.reshape(1, d)] + list(w)
    if slot_norm_w is not None:
        in_specs.append(pl.BlockSpec((1, LANES), lambda i: (0, 0)))
        args.append(slot_norm_w.reshape(1, LANES))
    return pl.pallas_call(
        functools.partial(_norm_matmul_kernel, splits=splits, slot_norm=slot_norm_w is not None, nw=len(w)),
        grid=(t // tile,),
        in_specs=in_specs,
        out_specs=[pl.BlockSpec((tile, s), lambda i: (i, 0)) for s in splits],
        out_shape=[jax.ShapeDtypeStruct((t, s), F32) for s in splits],
        compiler_params=_cparams("parallel"),
        name=name,
    )(*args)


def _pool_kernel(u_ref, sc_ref, *rest, tl, nw):
    w_refs, (o_ref, st_ref, hist_ref) = rest[:nw], rest[nw:]
    i = pl.program_id(1)

    @pl.when(i == 0)
    def _():
        hist_ref[...] = jnp.zeros_like(hist_ref)

    u = u_ref[...]
    ext = jnp.concatenate([hist_ref[...], u], axis=0)
    s2 = ext + pltpu.roll(ext, 1, 0)
    s4 = s2 + pltpu.roll(s2, 2, 0)
    s8 = s4 + pltpu.roll(s4, 4, 0)
    s16 = s8 + pltpu.roll(s8, 8, 0)
    lane = _lane((tl, GROUP_W))
    grp = lane // 64
    pooled = jnp.where(grp == 0, s2[16:], jnp.where(grp == 1, s4[16:], jnp.where(grp == 2, s8[16:], s16[16:])))
    win = jnp.where(grp == 0, 2, jnp.where(grp == 1, 4, jnp.where(grp == 2, 8, 16)))
    pos = i * tl + _row((tl, GROUP_W))
    cnt = jnp.minimum(win, pos + 1).astype(F32)
    d = pooled / cnt - u
    o_ref[...] = _mm(_act(d, nw == 2), _wts(w_refs)) * sc_ref[...]
    hist_ref[...] = ext[tl:]
    st_ref[0] = ext[tl:]


def pool_prompt(z_a, w_bd, scale, n_seq, seq_len, tl=512):
    nt = seq_len // tl
    return pl.pallas_call(
        functools.partial(_pool_kernel, tl=tl, nw=len(w_bd)),
        grid=(n_seq, nt),
        in_specs=[pl.BlockSpec((tl, GROUP_W), lambda s, i: (s * nt + i, 0)),
                  pl.BlockSpec((1, GROUP_W), lambda s, i: (0, 0))] + [_full(a, 2) for a in w_bd],
        out_specs=[pl.BlockSpec((tl, GROUP_W), lambda s, i: (s * nt + i, 0)),
                   pl.BlockSpec((1, 16, GROUP_W), lambda s, i: (s, 0, 0))],
        out_shape=[jax.ShapeDtypeStruct((n_seq * seq_len, GROUP_W), F32),
                   jax.ShapeDtypeStruct((n_seq, 16, GROUP_W), F32)],
        scratch_shapes=[pltpu.VMEM((16, GROUP_W), F32)],
        compiler_params=_cparams("parallel", "arbitrary"),
        name="pool_prompt",
    )(z_a, scale.reshape(1, GROUP_W), *w_bd)


def _conv_kernel(u_ref, cw_ref, cb_ref, lw_ref, lb_ref, pb_ref, *rest, tl, nw):
    w_refs, (o_ref, st_ref, ext_ref) = rest[:nw], rest[nw:]
    i = pl.program_id(1)

    @pl.when(i == 0)
    def _():
        ext_ref[0:32, :] = jnp.zeros((32, CONV_C), F32)

    u = u_ref[...]
    g = u[:, :CONV_C] * jax.nn.sigmoid(u[:, CONV_C:])
    ext_ref[32:, :] = g
    y = jnp.zeros((tl, CONV_C), F32)
    for j in range(CONV_W):
        y = y + ext_ref[2 + j:2 + j + tl, :] * cw_ref[j:j + 1, :]
    y = y + cb_ref[...]
    mu = jnp.mean(y, axis=-1, keepdims=True)
    yc = y - mu
    var = jnp.mean(yc * yc, axis=-1, keepdims=True)
    y = yc * lax.rsqrt(var + EPS) * lw_ref[...] + lb_ref[...]
    o_ref[...] = _mm(_act(jax.nn.silu(y), nw == 2), _wts(w_refs)) + pb_ref[...]
    tail = ext_ref[tl:tl + 32, :]
    st_ref[0] = tail
    ext_ref[0:32, :] = tail


def conv_prompt(z_d, cw, cb, lw, lb, pw, pb, n_seq, seq_len, tl=512):
    nt = seq_len // tl
    vec = lambda: pl.BlockSpec((1, CONV_C), lambda s, i: (0, 0))
    return pl.pallas_call(
        functools.partial(_conv_kernel, tl=tl, nw=len(pw)),
        grid=(n_seq, nt),
        in_specs=[pl.BlockSpec((tl, 2 * CONV_C), lambda s, i: (s * nt + i, 0)),
                  pl.BlockSpec((32, CONV_C), lambda s, i: (0, 0)),
                  vec(), vec(), vec(), vec()] + [_full(a, 2) for a in pw],
        out_specs=[pl.BlockSpec((tl, CONV_C), lambda s, i: (s * nt + i, 0)),
                   pl.BlockSpec((1, 32, CONV_C), lambda s, i: (s, 0, 0))],
        out_shape=[jax.ShapeDtypeStruct((n_seq * seq_len, CONV_C), F32),
                   jax.ShapeDtypeStruct((n_seq, 32, CONV_C), F32)],
        scratch_shapes=[pltpu.VMEM((tl + 32, CONV_C), F32)],
        compiler_params=_cparams("parallel", "arbitrary"),
        name="conv_prompt",
    )(z_d, cw, cb.reshape(1, -1), lw.reshape(1, -1), lb.reshape(1, -1), pb.reshape(1, -1), *pw)


def _split_pair(x):
    low = _lane(x.shape) < 64
    return jnp.where(low, x, 0.0), jnp.where(low, pltpu.roll(x, 64, 1), 0.0)


def _hgrn_chunk(zc, lb, onw, tri, s_ref, hp):
    c = zc.shape[0]
    nk = HG_HEADS * HG_DK
    q = zc[:, 0:nk]
    zf = zc[:, nk:2 * nk]
    v_all = zc[:, 2 * nk:2 * nk + 256]
    g_all = zc[:, 2 * nk + 256:2 * nk + 512]
    log_f = jnp.log(lb + (1.0 - lb) * jax.nn.sigmoid(zf))
    k = (1.0 - lb) * jax.nn.sigmoid(-zf)
    hi = log_f.astype(BF16)
    r1 = log_f - hi.astype(F32)
    mid = r1.astype(BF16)
    lo = (r1 - mid.astype(F32)).astype(BF16)
    b = _dot(tri, hi) + _dot(tri, mid) + _dot(tri, lo)
    b_last = b[c - 1:c, :]
    q_inter = q * jnp.exp(b)
    k_tail = k * jnp.exp(b_last - b)
    decay_last = jnp.exp(b_last)
    n_sub = c // HG_SUB
    rowc = _row((HG_SUB, LANES))
    outs = []
    for p in range(2):
        v_pair = _split_pair(v_all[:, LANES * p:LANES * (p + 1)])
        g_pair = _split_pair(g_all[:, LANES * p:LANES * (p + 1)])
        o_pair = []
        for hh in range(2):
            h = 2 * p + hh
            sl = slice(HG_DK * h, HG_DK * (h + 1))
            vh = v_pair[hh]
            vh_t = _act(vh, hp)
            s_old = s_ref[h]
            o = _mm(_act(q_inter[:, sl], hp), _act(s_old, hp))
            bh, qh, kh = b[:, sl], q[:, sl], k[:, sl]
            parts = []
            for ib in range(n_sub):
                r0 = ib * HG_SUB
                b_i, q_i = bh[r0:r0 + HG_SUB], qh[r0:r0 + HG_SUB]
                acc = o[r0:r0 + HG_SUB]
                if ib > 0:
                    ref_b = bh[r0 - 1:r0, :]
                    qt = q_i * jnp.exp(b_i - ref_b)
                    kt = kh[0:r0] * jnp.exp(ref_b - bh[0:r0])
                    a = _mm(_act(qt, hp), _act(kt, hp), nt=True)
                    acc = acc + _mm(_act(a, hp), tuple(t[0:r0] for t in vh_t))
                for s in range(HG_SUB):
                    r = r0 + s
                    e = jnp.exp(jnp.minimum(b_i - bh[r:r + 1, :], 0.0))
                    pr = jnp.where(rowc >= s, q_i * kh[r:r + 1, :] * e, 0.0)
                    acc = acc + jnp.sum(pr, axis=-1, keepdims=True) * vh[r:r + 1, :]
                parts.append(acc)
            o = jnp.concatenate(parts, axis=0) if n_sub > 1 else parts[0]
            dcol = jnp.broadcast_to(decay_last[:, sl], (LANES, LANES)).T
            s_ref[h] = dcol * s_old + _mm(_act(k_tail[:, sl].T, hp), vh_t)
            o = _rms(o, onw, HG_DV) * jax.nn.silu(g_pair[hh])
            o_pair.append(o)
        outs.append(o_pair[0] + pltpu.roll(o_pair[1], 64, 1))
    return jnp.concatenate(outs, axis=1)


def _hgrn_kernel(z_ref, lb_ref, onw_ref, tri_ref, o_ref, st_ref, s_ref, *, tl, hp):
    i = pl.program_id(1)

    @pl.when(i == 0)
    def _():
        s_ref[...] = jnp.zeros_like(s_ref)

    def body(cidx, carry):
        r0 = pl.multiple_of(cidx * HG_CHUNK, HG_CHUNK)
        zc = z_ref[pl.ds(r0, HG_CHUNK), :]
        o_ref[pl.ds(r0, HG_CHUNK), :] = _hgrn_chunk(zc, lb_ref[...], onw_ref[...], tri_ref[...], s_ref, hp)
        return carry

    lax.fori_loop(0, tl // HG_CHUNK, body, 0)
    st_ref[0] = s_ref[...]


def hgrn_prompt(z_c, lb, onw_slot, n_seq, seq_len, hp, tl=512):
    nt = seq_len // tl
    tri = jnp.tril(jnp.ones((HG_CHUNK, HG_CHUNK), F32)).astype(BF16)
    zw = z_c.shape[1]
    return pl.pallas_call(
        functools.partial(_hgrn_kernel, tl=tl, hp=hp),
        grid=(n_seq, nt),
        in_specs=[pl.BlockSpec((tl, zw), lambda s, i: (s * nt + i, 0)),
                  pl.BlockSpec((1, HG_HEADS * HG_DK), lambda s, i: (0, 0)),
                  pl.BlockSpec((1, LANES), lambda s, i: (0, 0)),
                  pl.BlockSpec((HG_CHUNK, HG_CHUNK), lambda s, i: (0, 0))],
        out_specs=[pl.BlockSpec((tl, GROUP_W), lambda s, i: (s * nt + i, 0)),
                   pl.BlockSpec((1, HG_HEADS, HG_DK, LANES), lambda s, i: (s, 0, 0, 0))],
        out_shape=[jax.ShapeDtypeStruct((n_seq * seq_len, GROUP_W), F32),
                   jax.ShapeDtypeStruct((n_seq, HG_HEADS, HG_DK, LANES), F32)],
        scratch_shapes=[pltpu.VMEM((HG_HEADS, HG_DK, LANES), F32)],
        compiler_params=_cparams("parallel", "arbitrary"),
        name="hgrn_prompt",
    )(z_c, lb.reshape(1, -1), onw_slot.reshape(1, LANES), tri)


def _mla_proj_kernel(z_ref, cs_ref, sn_ref, qnw_ref, qv_ref, qvs_ref, kvnw_ref, kv_ref, krv_ref, krvs_ref,
                     *rest, absorb, nw):
    wq_refs, wkv_refs, rest = rest[:nw], rest[nw:2 * nw], rest[2 * nw:]
    if absorb:
        wabs_ref, q_ref, k_ref, v_ref, ckv_ref, kr_ref, qabs_ref = rest
    else:
        q_ref, k_ref, v_ref, ckv_ref, kr_ref = rest
    hp = nw == 2
    z = z_ref[...]
    cos, sin = cs_ref[...], sn_ref[...]
    tm = z.shape[0]
    lane = _lane((tm, LANES))
    is_rope = lane < MLA_ROPE
    kr_raw = z[:, 384:512]
    kr_sw = z[:, 512:640]
    r_kr = lax.rsqrt(jnp.sum(kr_raw * kr_raw, axis=-1, keepdims=True) * (1.0 / MLA_ROPE) + EPS)
    k_rope = (kr_raw * r_kr * krv_ref[...]) * cos + (kr_sw * r_kr * krvs_ref[...]) * sin
    kr_ref[...] = k_rope
    c_kv = _rms(z[:, 256:384], kvnw_ref[...])
    ckv_ref[...] = c_kv
    kv = _mm(_act(c_kv, hp), _wts(wkv_refs))
    v_ref[...] = kv[:, 512:].astype(v_ref.dtype)
    ks = []
    for h in range(MLA_HEADS):
        kraw = kv[:, LANES * h:LANES * (h + 1)]
        ks.append(_rms(kraw, kv_ref[...], MLA_NOPE) + k_rope)
    k_ref[...] = jnp.concatenate(ks, axis=1).astype(k_ref.dtype)
    c_q = _rms(z[:, 0:256], qnw_ref[...])
    qq = _mm(_act(c_q, hp), _wts(wq_refs))
    qs = []
    for h in range(MLA_HEADS):
        x = qq[:, LANES * h:LANES * (h + 1)]
        xs = qq[:, 512 + LANES * h:512 + LANES * (h + 1)]
        x2 = x * x
        r_rope = lax.rsqrt(jnp.sum(jnp.where(is_rope, x2, 0.0), axis=-1, keepdims=True) * (1.0 / MLA_ROPE) + EPS)
        r_nope = lax.rsqrt(jnp.sum(jnp.where(is_rope, 0.0, x2), axis=-1, keepdims=True) * (1.0 / MLA_NOPE) + EPS)
        y = x * jnp.where(is_rope, r_rope, r_nope) * qv_ref[...]
        ysw = xs * r_rope * qvs_ref[...]
        qs.append((y * cos + ysw * sin) * MLA_SCALE)
    q = jnp.concatenate(qs, axis=1)
    q_ref[...] = q.astype(q_ref.dtype)
    if absorb:
        kvec4 = jnp.concatenate([kv_ref[...]] * MLA_HEADS, axis=1)
        qabs_ref[...] = _dot((q * kvec4).astype(BF16), wabs_ref[...])


def mla_proj(z_b, cos_t, sin_t, pw, row0, n_rows, absorb, hp, tile=TOK_TILE):
    assert row0 % tile == 0 and n_rows % tile == 0
    t0 = row0 // tile
    rows = lambda w: pl.BlockSpec((tile, w), lambda i: (i + t0, 0))
    out_rows = lambda w: pl.BlockSpec((tile, w), lambda i: (i, 0))
    consts = [pw["q_norm_w"], pw["qvec"], pw["qvec_sw"], pw["kv_norm_w"], pw["kvec"], pw["krvec"], pw["krvec_sw"]]
    consts += pw["wq"] + pw["wkv"]
    if absorb:
        consts.append(pw["wabs"])
    qkv_dt = F32 if hp else BF16
    out_shape = [jax.ShapeDtypeStruct((n_rows, 512), qkv_dt)] * 3 + [jax.ShapeDtypeStruct((n_rows, LANES), F32)] * 2
    out_specs = [out_rows(512)] * 3 + [out_rows(LANES)] * 2
    if absorb:
        out_shape.append(jax.ShapeDtypeStruct((n_rows, 512), F32))
        out_specs.append(out_rows(512))
    return pl.pallas_call(
        functools.partial(_mla_proj_kernel, absorb=absorb, nw=len(pw["wq"])),
        grid=(n_rows // tile,),
        in_specs=[rows(640), rows(LANES), rows(LANES)] + [_full(a, 1) for a in consts],
        out_specs=out_specs,
        out_shape=out_shape,
        compiler_params=_cparams("parallel"),
        name="mla_proj_s" if absorb else "mla_proj_p",
    )(z_b, cos_t, sin_t, *consts)


def _flash_kernel(qi_ref, kj_ref, q_ref, k_ref, v_ref, o_ref, m_ref, l_ref, acc_ref, *, tq, hp):
    n = pl.program_id(1)
    qi, kj = qi_ref[n], kj_ref[n]

    @pl.when(kj == 0)
    def _():
        m_ref[...] = jnp.full_like(m_ref, -jnp.inf)
        l_ref[...] = jnp.zeros_like(l_ref)
        acc_ref[...] = jnp.zeros_like(acc_ref)

    def operand(ref, sl):
        x = ref[:, sl]
        return _split(x) if hp else (x,)

    def step(masked):
        for h in range(MLA_HEADS):
            sl = slice(LANES * h, LANES * (h + 1))
            s = _mm(operand(q_ref, sl), operand(k_ref, sl), nt=True)
            if masked:
                s = jnp.where(_row((tq, tq)) >= _lane((tq, tq)), s, -jnp.inf)
            m_prev = m_ref[h]
            m_new = jnp.maximum(m_prev, jnp.max(s, axis=-1, keepdims=True))
            alpha = jnp.exp(m_prev - m_new)
            p = jnp.exp(s - jnp.tile(m_new, (1, tq // LANES)))
            l_ref[h] = alpha * l_ref[h] + jnp.sum(p, axis=-1, keepdims=True)
            acc_ref[h] = alpha * acc_ref[h] + _mm(_act(p, hp), operand(v_ref, sl))
            m_ref[h] = m_new

    @pl.when(kj < qi)
    def _():
        step(False)

    @pl.when(kj == qi)
    def _():
        step(True)
        o_ref[...] = jnp.concatenate([acc_ref[h] / l_ref[h] for h in range(MLA_HEADS)], axis=1)


def flash_prompt(q, k, v, n_seq, seq_len, hp, tq=512):
    nq = seq_len // tq
    qi = np.array([i for i in range(nq) for _ in range(i + 1)], np.int32)
    kj = np.array([j for i in range(nq) for j in range(i + 1)], np.int32)
    grid_spec = pltpu.PrefetchScalarGridSpec(
        num_scalar_prefetch=2,
        grid=(n_seq, len(qi)),
        in_specs=[pl.BlockSpec((tq, 512), lambda b, n, qi, kj: (b * nq + qi[n], 0)),
                  pl.BlockSpec((tq, 512), lambda b, n, qi, kj: (b * nq + kj[n], 0)),
                  pl.BlockSpec((tq, 512), lambda b, n, qi, kj: (b * nq + kj[n], 0))],
        out_specs=pl.BlockSpec((tq, 512), lambda b, n, qi, kj: (b * nq + qi[n], 0)),
        scratch_shapes=[pltpu.VMEM((MLA_HEADS, tq, LANES), F32), pltpu.VMEM((MLA_HEADS, tq, LANES), F32),
                        pltpu.VMEM((MLA_HEADS, tq, LANES), F32)])
    return pl.pallas_call(
        functools.partial(_flash_kernel, tq=tq, hp=hp),
        grid_spec=grid_spec,
        out_shape=jax.ShapeDtypeStruct((n_seq * seq_len, 512), F32),
        compiler_params=_cparams("parallel", "arbitrary"),
        name="mla_flash_prompt",
    )(jnp.asarray(qi), jnp.asarray(kj), q, k, v)


def _mla_sample_kernel(pt_ref, *refs, npg, n_steps):
    c_refs = refs[:npg]
    r_refs = refs[npg:2 * npg]
    qa_ref, qr_ref, cn_ref, rn_ref, wuk_ref, seg_ref, o_ref, m_ref, l_ref, acc_ref = refs[2 * npg:]
    g = pl.program_id(1)

    @pl.when(g == 0)
    def _():
        m_ref[...] = jnp.full_like(m_ref, -jnp.inf)
        l_ref[...] = jnp.zeros_like(l_ref)
        acc_ref[...] = jnp.zeros_like(acc_ref)

    qa = qa_ref[...].astype(BF16)
    qr = qr_ref[...][:, :MLA_ROPE].astype(BF16)
    wuk = wuk_ref[...]
    seg = seg_ref[...]

    def nope_scores(cb):
        kraw = _dot(cb, wuk)
        ssq = _dot_nt(seg, (kraw * kraw).astype(BF16))
        return _dot_nt(qa, cb) * lax.rsqrt(ssq * (1.0 / MLA_NOPE) + EPS)

    def update(s_all, cb):
        m_prev = m_ref[...]
        m_new = jnp.maximum(m_prev, jnp.max(s_all, axis=-1, keepdims=True))
        alpha = jnp.exp(m_prev - m_new)
        p_all = jnp.exp(s_all - m_new)
        l_ref[...] = alpha * l_ref[...] + jnp.sum(p_all, axis=-1, keepdims=True)
        acc_ref[...] = alpha * acc_ref[...] + _dot(p_all.astype(BF16), cb)
        m_ref[...] = m_new

    cb_all = jnp.concatenate([c_refs[p][...].astype(BF16) for p in range(npg)], axis=0)
    kr_all = jnp.concatenate([r_refs[p][...].astype(BF16) for p in range(npg)], axis=1)
    update(nope_scores(cb_all) + _dot(qr, kr_all), cb_all)

    @pl.when(g == n_steps - 1)
    def _():
        cb = cn_ref[...].astype(BF16)
        s_new = nope_scores(cb) + _dot_nt(qr, rn_ref[...][:, :MLA_ROPE].astype(BF16))
        qtok = _row(s_new.shape) // MLA_HEADS
        s_new = jnp.where(_lane(s_new.shape) <= qtok, s_new, -jnp.inf)
        update(s_new, cb)
        o_ref[...] = acc_ref[...] / l_ref[...]


def mla_sample(page_table, cache_c, cache_rt, layer, qa, qr, c_new, r_new, wuk, seg):
    nb, n_pages = page_table.shape
    npg = PAGES_PER_STEP
    n_steps = n_pages // npg
    rows = qa.shape[1]
    s_pad = c_new.shape[1]

    def page_spec(shape, p):
        return pl.BlockSpec((None, None) + shape, lambda b, g, pt: (layer, pt[b * n_pages + g * npg + p], 0, 0))

    per_seq = lambda r, w: pl.BlockSpec((None, r, w), lambda b, g, pt: (b, 0, 0))
    full = lambda a: pl.BlockSpec(a.shape, lambda b, g, pt: (0,) * a.ndim)
    grid_spec = pltpu.PrefetchScalarGridSpec(
        num_scalar_prefetch=1,
        grid=(nb, n_steps),
        in_specs=[page_spec((PAGE_SIZE, MLA_KV_LORA), p) for p in range(npg)]
                 + [page_spec((MLA_ROPE, PAGE_SIZE), p) for p in range(npg)]
                 + [per_seq(rows, LANES), per_seq(rows, LANES), per_seq(s_pad, LANES), per_seq(s_pad, LANES),
                    full(wuk), full(seg)],
        out_specs=per_seq(rows, LANES),
        scratch_shapes=[pltpu.VMEM((rows, 1), F32), pltpu.VMEM((rows, 1), F32), pltpu.VMEM((rows, LANES), F32)])
    return pl.pallas_call(
        functools.partial(_mla_sample_kernel, npg=npg, n_steps=n_steps),
        grid_spec=grid_spec,
        out_shape=jax.ShapeDtypeStruct((nb, rows, LANES), F32),
        compiler_params=_cparams("parallel", "arbitrary"),
        name="mla_sample",
    )(page_table.reshape(-1), *([cache_c] * npg), *([cache_rt] * npg), qa, qr, c_new, r_new, wuk, seg)


def _matmul_kernel(a_ref, w_ref, o_ref):
    o_ref[...] = _dot(a_ref[...].astype(BF16), w_ref[...])


def matmul(a, w, tile=TOK_TILE, name="matmul"):
    t, kdim = a.shape
    n = w.shape[1]
    return pl.pallas_call(
        _matmul_kernel,
        grid=(t // tile,),
        in_specs=[pl.BlockSpec((tile, kdim), lambda i: (i, 0)), pl.BlockSpec((kdim, n), lambda i: (0, 0))],
        out_specs=pl.BlockSpec((tile, n), lambda i: (i, 0)),
        out_shape=jax.ShapeDtypeStruct((t, n), F32),
        compiler_params=_cparams("parallel"),
        name=name,
    )(a, w)


def _mix_out_kernel(oa_ref, ob_ref, oc_ref, od_ref, x_ref, ga_ref, gb_ref, gc_ref, gd_ref, n2_ref, *rest, nw):
    w_groups = [rest[nw * g:nw * (g + 1)] for g in range(5)]
    x1_ref, q_ref = rest[5 * nw:]
    hp = nw == 2
    acc = x_ref[...]
    for o_ref, g_ref, w_refs in ((oa_ref, ga_ref, w_groups[0]), (ob_ref, gb_ref, w_groups[1]),
                                 (oc_ref, gc_ref, w_groups[2]), (od_ref, gd_ref, w_groups[3])):
        acc = acc + _mm(_act(_rms(o_ref[...], g_ref[...], GROUP_W), hp), _wts(w_refs))
    x1_ref[...] = acc
    q_ref[...] = _mm(_act(_rms(acc, n2_ref[...]), hp), _wts(w_groups[4]))


def mix_out(o_a, o_b, o_c, o_d, x, gw, w_out, n2w, wq, tile=TOK_TILE):
    t = x.shape[0]
    rows = lambda a: pl.BlockSpec((tile, a.shape[1]), lambda i: (i, 0))
    acts = [o_a, o_b, o_c, o_d, x]
    consts = list(gw) + [n2w.reshape(1, -1)] + [a for w in w_out for a in w] + list(wq)
    return pl.pallas_call(
        functools.partial(_mix_out_kernel, nw=len(wq)),
        grid=(t // tile,),
        in_specs=[rows(a) for a in acts] + [_full(a, 1) for a in consts],
        out_specs=[pl.BlockSpec((tile, D_MODEL), lambda i: (i, 0)), pl.BlockSpec((tile, CA_W), lambda i: (i, 0))],
        out_shape=[jax.ShapeDtypeStruct((t, D_MODEL), F32), jax.ShapeDtypeStruct((t, CA_W), F32)],
        compiler_params=_cparams("parallel"),
        name="mix_out",
    )(*acts, *consts)


def _cross_kernel(q_ref, mk_ref, mv_ref, x_ref, qn_ref, *rest, nw):
    wo_refs, (o_ref,) = rest[:nw], rest[nw:]
    hp = nw == 2
    q = q_ref[...]
    outs = []
    for h in range(CA_HEADS):
        sl = slice(CA_HD * h, CA_HD * (h + 1))
        qh = _rms(q[:, sl], qn_ref[...])
        s = _mm(_act(qh, hp), _act(mk_ref[:, sl], hp), nt=True) * (CA_HD ** -0.5)
        s = s - jnp.max(s, axis=-1, keepdims=True)
        e = jnp.exp(s)
        p = e / jnp.sum(e, axis=-1, keepdims=True)
        outs.append(_mm(_act(p, hp), _act(mv_ref[:, sl], hp)))
    o = jnp.concatenate(outs, axis=1)
    o_ref[...] = x_ref[...] + _mm(_act(o, hp), _wts(wo_refs))


def cross_attend(q, mem_k, mem_v, x, qnw, wo, tl):
    n_seq, seq_len, _ = q.shape
    nt = seq_len // tl
    return pl.pallas_call(
        functools.partial(_cross_kernel, nw=len(wo)),
        grid=(n_seq, nt),
        in_specs=[pl.BlockSpec((None, tl, CA_W), lambda s, i: (s, i, 0)),
                  pl.BlockSpec((None, N_MEM, CA_W), lambda s, i: (s, 0, 0)),
                  pl.BlockSpec((None, N_MEM, CA_W), lambda s, i: (s, 0, 0)),
                  pl.BlockSpec((None, tl, D_MODEL), lambda s, i: (s, i, 0)),
                  pl.BlockSpec((1, CA_HD), lambda s, i: (0, 0))] + [_full(a, 2) for a in wo],
        out_specs=pl.BlockSpec((None, tl, D_MODEL), lambda s, i: (s, i, 0)),
        out_shape=jax.ShapeDtypeStruct((n_seq, seq_len, D_MODEL), F32),
        compiler_params=_cparams("parallel", "parallel"),
        name="cross_attend",
    )(q, mem_k, mem_v, x, qnw.reshape(1, CA_HD), *wo)


def _swiglu_into(h, wg_refs, wu_refs, wd_refs, o_ref, width):
    hp = len(wg_refs) == 2
    for c0 in range(0, width, FF_CHUNK):
        cols = (slice(None), slice(c0, c0 + FF_CHUNK))
        g = _mm(h, _wts(wg_refs, cols))
        u = _mm(h, _wts(wu_refs, cols))
        a = _act(jax.nn.silu(g) * u, hp)
        o_ref[...] += _mm(a, _wts(wd_refs, (slice(c0, c0 + FF_CHUNK), slice(None))))


def _ffn_kernel(x_ref, nw_ref, *rest, nw, width):
    wg, wu, wd, (o_ref,) = rest[:nw], rest[nw:2 * nw], rest[2 * nw:3 * nw], rest[3 * nw:]
    x = x_ref[...]

    @pl.when(pl.program_id(1) == 0)
    def _():
        o_ref[...] = x

    _swiglu_into(_act(_rms(x, nw_ref[...]), nw == 2), wg, wu, wd, o_ref, width)


def ffn_dense(x, nw, wg, wu, wd, ff_block, tile=TOK_TILE):
    t = x.shape[0]
    up = lambda: pl.BlockSpec((D_MODEL, ff_block), lambda i, j: (0, j))
    down = lambda: pl.BlockSpec((ff_block, D_MODEL), lambda i, j: (j, 0))
    n = len(wg)
    return pl.pallas_call(
        functools.partial(_ffn_kernel, nw=n, width=ff_block),
        grid=(t // tile, D_FF // ff_block),
        in_specs=[pl.BlockSpec((tile, D_MODEL), lambda i, j: (i, 0)), pl.BlockSpec((1, D_MODEL), lambda i, j: (0, 0))]
                 + [up() for _ in range(2 * n)] + [down() for _ in range(n)],
        out_specs=pl.BlockSpec((tile, D_MODEL), lambda i, j: (i, 0)),
        out_shape=jax.ShapeDtypeStruct((t, D_MODEL), F32),
        compiler_params=_cparams("parallel", "arbitrary"),
        name="ffn_dense",
    )(x, nw.reshape(1, -1), *wg, *wu, *wd)


def _router_kernel(x_ref, nw_ref, r_ref, h_ref, g_ref):
    h = _rms(x_ref[...], nw_ref[...])
    h_ref[...] = h.astype(BF16)
    logits = jnp.dot(h, r_ref[...], preferred_element_type=F32, precision=lax.Precision.HIGHEST)
    lane = _lane(logits.shape)
    neg = -jnp.inf
    l1 = jnp.where(lane < N_EXPERTS, logits, neg)
    m1 = jnp.max(l1, axis=-1, keepdims=True)
    i1 = jnp.min(jnp.where(l1 == m1, lane, LANES), axis=-1, keepdims=True)
    l2 = jnp.where(lane == i1, neg, l1)
    m2 = jnp.max(l2, axis=-1, keepdims=True)
    i2 = jnp.min(jnp.where(l2 == m2, lane, LANES), axis=-1, keepdims=True)
    e2 = jnp.exp(m2 - m1)
    den = 1.0 + e2
    g_ref[...] = jnp.where(lane == i1, 1.0 / den, jnp.where(lane == i2, e2 / den, 0.0))


def moe_router(x, nw, router_pad, tile=TOK_TILE):
    t = x.shape[0]
    return pl.pallas_call(
        _router_kernel,
        grid=(t // tile,),
        in_specs=[pl.BlockSpec((tile, D_MODEL), lambda i: (i, 0)), pl.BlockSpec((1, D_MODEL), lambda i: (0, 0)),
                  pl.BlockSpec((D_MODEL, LANES), lambda i: (0, 0))],
        out_specs=[pl.BlockSpec((tile, D_MODEL), lambda i: (i, 0)), pl.BlockSpec((tile, LANES), lambda i: (i, 0))],
        out_shape=[jax.ShapeDtypeStruct((t, D_MODEL), BF16), jax.ShapeDtypeStruct((t, LANES), F32)],
        compiler_params=_cparams("parallel"),
        name="moe_router",
    )(x, nw.reshape(1, -1), router_pad)


def _experts_kernel(be_ref, nv_ref, x_ref, wg_ref, wu_ref, wd_ref, o_ref):
    b = pl.program_id(0)
    o_ref[...] = jnp.zeros_like(o_ref)

    @pl.when(b < nv_ref[0])
    def _():
        _swiglu_into((x_ref[...],), (wg_ref,), (wu_ref,), (wd_ref,), o_ref, D_FF)


def moe_experts(buf, blk_e, n_valid, wg, wu, wd, tile=MOE_TILE):
    p = buf.shape[0]
    wspec = lambda a: pl.BlockSpec((None,) + a.shape[1:], lambda b, be, nv: (be[b], 0, 0))
    grid_spec = pltpu.PrefetchScalarGridSpec(
        num_scalar_prefetch=2,
        grid=(p // tile,),
        in_specs=[pl.BlockSpec((tile, D_MODEL), lambda b, be, nv: (b, 0)), wspec(wg), wspec(wu), wspec(wd)],
        out_specs=pl.BlockSpec((tile, D_MODEL), lambda b, be, nv: (b, 0)))
    return pl.pallas_call(
        _experts_kernel,
        grid_spec=grid_spec,
        out_shape=jax.ShapeDtypeStruct((p, D_MODEL), F32),
        compiler_params=_cparams("arbitrary"),
        name="moe_experts",
    )(blk_e, n_valid, buf, wg, wu, wd)


def moe_ffn(x, nw, router_pad, wg, wu, wd):
    t = x.shape[0]
    h, gates = moe_router(x, nw, router_pad)
    gates = gates[:, :N_EXPERTS]
    sel = gates > 0.0
    pos = jnp.cumsum(sel.astype(jnp.int32), axis=0) - 1
    counts = pos[-1] + 1
    nb_e = (counts + MOE_TILE - 1) // MOE_TILE
    blk_end = jnp.cumsum(nb_e)
    row_start = (blk_end - nb_e) * MOE_TILE
    slot = row_start[None, :] + pos
    n_blk = -(-(2 * t + N_EXPERTS * (MOE_TILE - 1)) // MOE_TILE)
    p_rows = n_blk * MOE_TILE
    tok = jnp.broadcast_to(jnp.arange(t, dtype=jnp.int32)[:, None], slot.shape)
    src = jnp.zeros((p_rows,), jnp.int32).at[jnp.where(sel, slot, p_rows)].set(tok, mode="drop")
    buf = h[src]
    blk_e = jnp.minimum(jnp.searchsorted(blk_end, jnp.arange(n_blk, dtype=jnp.int32), side="right"),
                        N_EXPERTS - 1).astype(jnp.int32)
    y_buf = moe_experts(buf, blk_e, blk_end[-1:].astype(jnp.int32), wg, wu, wd)
    g2, e2 = lax.top_k(gates, 2)
    s2 = jnp.take_along_axis(slot, e2, axis=1)
    y = y_buf[s2[:, 0]] * g2[:, 0:1] + y_buf[s2[:, 1]] * g2[:, 1:2]
    return x + y


def _zeros(r, c):
    return jnp.zeros((r, c), F32)


def _prep_w_in(w, hp):
    d = w.shape[0]
    k_r = w[:, 640:672]
    kr_slot = jnp.concatenate([k_r, _zeros(d, 96)], axis=1)
    kr_sw = jnp.concatenate([k_r[:, 16:], k_r[:, :16], _zeros(d, 96)], axis=1)
    return _w(jnp.concatenate([w[:, 0:256], w[:, 256:640], kr_slot, kr_sw, w[:, 672:2208], w[:, 2208:2720]],
                              axis=1), hp)


W_IN_SPLITS = (256, 640, 1536, 512)


def _slot_vec(rope_w, nope_w):
    z = jnp.zeros((32,), F32)
    return jnp.concatenate([rope_w, nope_w, z]).reshape(1, LANES)


def _prep_mla(lw, hp):
    wq, wkv = lw["mla_w_uq"], lw["mla_w_ukv"]
    slots, sw = [], []
    for h in range(MLA_HEADS):
        nope = wq[:, 96 * h:96 * h + 64]
        rope = wq[:, 96 * h + 64:96 * h + 96]
        slots.append(jnp.concatenate([rope, nope, _zeros(MLA_Q_LORA, 32)], axis=1))
        sw.append(jnp.concatenate([rope[:, 16:], rope[:, :16], _zeros(MLA_Q_LORA, 96)], axis=1))
    kslots = [jnp.concatenate([_zeros(128, 32), wkv[:, 128 * h:128 * h + 64], _zeros(128, 32)], axis=1)
              for h in range(MLA_HEADS)]
    vslots = [jnp.concatenate([wkv[:, 128 * h + 64:128 * h + 128], _zeros(128, 64)], axis=1)
              for h in range(MLA_HEADS)]
    z96 = jnp.zeros((96,), F32)
    qr, kr = lw["mla_qn_rope_w"], lw["mla_kn_rope_w"]
    wabs = jnp.zeros((512, 512), F32)
    for h in range(MLA_HEADS):
        wabs = wabs.at[128 * h + 32:128 * h + 96, 128 * h:128 * h + 128].set(wkv[:, 128 * h:128 * h + 64].T)
    wuv_bd = jnp.zeros((512, 256), F32)
    for h in range(MLA_HEADS):
        wuv_bd = wuv_bd.at[128 * h:128 * h + 128, 64 * h:64 * h + 64].set(wkv[:, 128 * h + 64:128 * h + 128])
    return dict(
        q_norm_w=lw["mla_q_norm_w"].reshape(1, -1),
        wq=_w(jnp.concatenate(slots + sw, axis=1), hp),
        qvec=_slot_vec(qr, lw["mla_qn_nope_w"]),
        qvec_sw=jnp.concatenate([qr[16:], qr[:16], z96]).reshape(1, LANES),
        kv_norm_w=lw["mla_kv_norm_w"].reshape(1, -1),
        wkv=_w(jnp.concatenate(kslots + vslots, axis=1), hp),
        kvec=_slot_vec(jnp.zeros((32,), F32), lw["mla_kn_nope_w"]),
        krvec=jnp.concatenate([kr, z96]).reshape(1, LANES),
        krvec_sw=jnp.concatenate([kr[16:], kr[:16], z96]).reshape(1, LANES),
        wabs=wabs.astype(BF16),
        wuk=jnp.concatenate([wkv[:, 128 * h:128 * h + 64] for h in range(MLA_HEADS)], axis=1).astype(BF16),
        wuv_bd=wuv_bd.astype(BF16),
    )


def _rope_tables(pos):
    half = MLA_ROPE // 2
    inv = 1.0 / (ROPE_THETA ** (jnp.arange(half, dtype=F32) / half))
    ang = pos.astype(F32)[:, None] * inv[None, :]
    cos, sin = jnp.cos(ang), jnp.sin(ang)
    n = pos.shape[0]
    cos_t = jnp.concatenate([cos, cos, jnp.ones((n, 64), F32), jnp.zeros((n, 32), F32)], axis=1)
    sin_t = jnp.concatenate([-sin, sin, jnp.zeros((n, 96), F32)], axis=1)
    return cos_t, sin_t


def _pad_new(a):
    n, s, w = a.shape
    return jnp.concatenate([a, jnp.zeros((n, 8 - s, w), a.dtype)], axis=1)


def _pad_rows(w, rows_per, pad_to):
    g = w.shape[0] // rows_per
    w = w.reshape(g, rows_per, -1)
    return jnp.concatenate([w, jnp.zeros((g, pad_to - rows_per, w.shape[-1]), w.dtype)], axis=1).reshape(
        g * pad_to, -1)


def _sample_pool(u, prefix, w_bd, scale):
    n, L, c = u.shape
    ext = jnp.concatenate([prefix, u], axis=1)
    cs = jnp.concatenate([jnp.zeros((n, 1, c), F32), jnp.cumsum(ext, axis=1)], axis=1)
    hi = cs[:, POOL_PAD + 1:]
    pooled = []
    for g, w in enumerate(POOL_WINDOWS):
        sl = slice(64 * g, 64 * (g + 1))
        lo = cs[:, POOL_PAD + 1 - w:POOL_PAD + 1 - w + L, sl]
        pooled.append((hi[..., sl] - lo) / float(w))
    d = jnp.concatenate(pooled, axis=-1) - u
    y = jnp.dot(d.reshape(n * L, c).astype(BF16), w_bd, preferred_element_type=F32) * scale
    return y, ext[:, -POOL_PAD:]


def _sample_hgrn(zc, s0, lb, onw):
    n, L, _ = zc.shape
    q = zc[..., 0:512].reshape(n, L, HG_HEADS, HG_DK)
    zf = zc[..., 512:1024].reshape(n, L, HG_HEADS, HG_DK)
    v = zc[..., 1024:1280].reshape(n, L, HG_HEADS, HG_DV)
    g = zc[..., 1280:1536].reshape(n, L, HG_HEADS, HG_DV)
    lbh = lb.reshape(HG_HEADS, HG_DK)
    f = lbh + (1.0 - lbh) * jax.nn.sigmoid(zf)
    k = (1.0 - lbh) * jax.nn.sigmoid(-zf)
    s = s0
    outs = []
    for t in range(L):
        s = f[:, t][..., None] * s + k[:, t][..., None] * v[:, t][:, :, None, :]
        outs.append(jnp.sum(s * q[:, t][..., None], axis=2))
    o = jnp.stack(outs, axis=1)
    o = _rms(o, onw, HG_DV) * jax.nn.silu(g)
    return o.reshape(n * L, GROUP_W), s


def _sample_conv(u, prefix, cw, cb, lw, lb, pw, pb):
    n, L, _ = u.shape
    g = u[..., :CONV_C] * jax.nn.sigmoid(u[..., CONV_C:])
    ext = jnp.concatenate([prefix, g], axis=1)
    y = sum(ext[:, j:j + L, :] * cw[j][None, None, :] for j in range(CONV_W)) + cb
    mu = jnp.mean(y, axis=-1, keepdims=True)
    yc = y - mu
    var = jnp.mean(yc * yc, axis=-1, keepdims=True)
    y = yc * lax.rsqrt(var + EPS) * lw + lb
    y = jnp.dot(jax.nn.silu(y).reshape(n * L, CONV_C).astype(BF16), pw, preferred_element_type=F32) + pb
    return y, ext[:, -CONV_PAD:]


def kernel(x_prompt, x_sample, cache_kv_latent, cache_k_rope, cache_mem_k, cache_mem_v, state_pool, state_hgrn, state_conv, page_table, mem_prompt, norm1_w, w_in, pool_w, pool_scale, mla_q_norm_w, mla_w_uq, mla_kv_norm_w, mla_w_ukv, mla_qn_nope_w, mla_qn_rope_w, mla_kn_nope_w, mla_kn_rope_w, hg_lb_param, hg_onorm_w, conv_w, conv_b, conv_ln_w, conv_ln_b, conv_pw_w, conv_pw_b, grp_norm_w, w_out, norm2_w, mem_norm_w, ca_wq, ca_wk, ca_wv, ca_qn_w, ca_kn_w, ca_wo, norm3_w, ffn_w_gate, ffn_w_up, ffn_w_down, moe_router, moe_w_gate, moe_w_up, moe_w_down):
    bp, seq, d = x_prompt.shape
    nb, s_len, _ = x_sample.shape
    depth = w_in.shape[0]
    tp, ts = bp * seq, nb * s_len
    n_past = page_table.shape[1] * PAGE_SIZE
    n_mem = mem_prompt.shape[1]
    first_moe = 1

    sm = jax.nn.softmax(hg_lb_param.astype(F32), axis=0)
    lower_bounds = jnp.cumsum(sm, axis=0) - sm[:1]

    pos = jnp.concatenate([jnp.tile(jnp.arange(seq, dtype=jnp.int32), bp),
                           jnp.tile(n_past + jnp.arange(s_len, dtype=jnp.int32), nb)])
    cos_t, sin_t = _rope_tables(pos)
    seg = (jnp.arange(256)[None, :] // 64 == jnp.arange(MLA_HEADS * s_len)[:, None] % MLA_HEADS).astype(BF16)
    cache_rt = jnp.swapaxes(cache_k_rope, 2, 3)

    x = jnp.concatenate([x_prompt.reshape(tp, d), x_sample.reshape(ts, d)], axis=0)
    mem2d = mem_prompt.reshape(bp * n_mem, d)
    outs = [[] for _ in range(12)]

    for l in range(depth):
        hp = l <= first_moe
        lw = dict(mla_q_norm_w=mla_q_norm_w[l], mla_w_uq=mla_w_uq[l], mla_kv_norm_w=mla_kv_norm_w[l],
                  mla_w_ukv=mla_w_ukv[l], mla_qn_nope_w=mla_qn_nope_w[l], mla_qn_rope_w=mla_qn_rope_w[l],
                  mla_kn_nope_w=mla_kn_nope_w[l], mla_kn_rope_w=mla_kn_rope_w[l])
        pw = _prep_mla(lw, hp)
        z_a, z_b, z_c, z_d = norm_matmul(x, norm1_w[l], _prep_w_in(w_in[l], hp), W_IN_SPLITS, name="norm_w_in")

        w_bd = _w(jax.scipy.linalg.block_diag(*[pool_w[l, g] for g in range(4)]), hp)
        o_a_p, pool_st = pool_prompt(z_a, w_bd, pool_scale[l], bp, seq)
        o_a_s, pool_st_s = _sample_pool(z_a[tp:].reshape(nb, s_len, GROUP_W), state_pool[l], w_bd[0], pool_scale[l])
        outs[0].append(pool_st[:, 1:])
        outs[7].append(pool_st_s)

        q_p, k_p, v_p, ckv_p, kr_p = mla_proj(z_b, cos_t, sin_t, pw, 0, tp, absorb=False, hp=hp)
        o_b_p = flash_prompt(q_p, k_p, v_p, bp, seq, hp)
        q_s, _, _, ckv_s, kr_s, qabs_s = mla_proj(z_b, cos_t, sin_t, pw, tp, ts, absorb=True, hp=hp)
        rows = MLA_HEADS * s_len
        o_lat = mla_sample(page_table, cache_kv_latent, cache_rt, l,
                           qabs_s.reshape(nb, rows, LANES), q_s.astype(F32).reshape(nb, rows, LANES),
                           _pad_new(ckv_s.reshape(nb, s_len, LANES)), _pad_new(kr_s.reshape(nb, s_len, LANES)),
                           pw["wuk"], seg)
        o_b_s = matmul(o_lat.reshape(ts, 512), pw["wuv_bd"], name="mla_v_up")
        o_b_s = jnp.concatenate([o_b_s.reshape(ts, 4, 64), jnp.zeros((ts, 4, 64), F32)], axis=-1).reshape(ts, 512)
        outs[1].append(ckv_p.reshape(bp, seq, MLA_KV_LORA))
        outs[2].append(kr_p[:, :MLA_ROPE].reshape(bp, seq, MLA_ROPE))
        outs[8].append(ckv_s.reshape(nb, s_len, MLA_KV_LORA))
        outs[9].append(kr_s[:, :MLA_ROPE].reshape(nb, s_len, MLA_ROPE))

        onw_slot = jnp.concatenate([hg_onorm_w[l], jnp.zeros((64,), F32)])
        o_c_p, hg_st = hgrn_prompt(z_c, lower_bounds[l], onw_slot, bp, seq, hp)
        o_c_s, hg_st_s = _sample_hgrn(z_c[tp:].reshape(nb, s_len, -1), state_hgrn[l], lower_bounds[l], hg_onorm_w[l])
        outs[3].append(hg_st[..., :HG_DV])
        outs[10].append(hg_st_s)

        cw_pad = jnp.concatenate([conv_w[l], jnp.zeros((1, CONV_C), F32)], axis=0)
        pw_c = _w(conv_pw_w[l], hp)
        o_d_p, conv_st = conv_prompt(z_d, cw_pad, conv_b[l], conv_ln_w[l], conv_ln_b[l], pw_c, conv_pw_b[l], bp, seq)
        o_d_s, conv_st_s = _sample_conv(z_d[tp:].reshape(nb, s_len, -1), state_conv[l], conv_w[l], conv_b[l],
                                        conv_ln_w[l], conv_ln_b[l], pw_c[0], conv_pw_b[l])
        outs[4].append(conv_st[:, 2:])
        outs[11].append(conv_st_s)

        o_a = jnp.concatenate([o_a_p, o_a_s], axis=0)
        o_b = jnp.concatenate([o_b_p, o_b_s], axis=0)
        o_c = jnp.concatenate([o_c_p, o_c_s], axis=0)
        o_d = jnp.concatenate([o_d_p, o_d_s], axis=0)
        gnw = grp_norm_w[l]
        gw = [gnw[0:256].reshape(1, -1), _pad_rows(gnw[256:512].reshape(-1, 1), 64, 128).reshape(1, -1),
              gnw[512:768].reshape(1, -1), gnw[768:1024].reshape(1, -1)]
        wo_l = w_out[l]
        w_out_parts = [_w(wo_l[0:256], hp), _w(_pad_rows(wo_l[256:512], 64, 128), hp),
                       _w(wo_l[512:768], hp), _w(wo_l[768:1024], hp)]
        x1, q_ca = mix_out(o_a, o_b, o_c, o_d, x, gw, w_out_parts, norm2_w[l], _w(ca_wq[l], hp))

        w_kv = _w(jnp.concatenate([ca_wk[l], ca_wv[l]], axis=1), hp)
        mk_p, mv_p = norm_matmul(mem2d, mem_norm_w[l], w_kv, (CA_W, CA_W), slot_norm_w=ca_kn_w[l],
                                 tile=min(TOK_TILE, mem2d.shape[0]), name="memory_kv")
        outs[5].append(mk_p.reshape(bp, n_mem, CA_HEADS, CA_HD))
        outs[6].append(mv_p.reshape(bp, n_mem, CA_HEADS, CA_HD))

        wo_ca = _w(ca_wo[l], hp)
        x2_p = cross_attend(q_ca[:tp].reshape(bp, seq, CA_W), mk_p.reshape(bp, n_mem, CA_W),
                            mv_p.reshape(bp, n_mem, CA_W), x1[:tp].reshape(bp, seq, d), ca_qn_w[l], wo_ca,
                            tl=min(512, seq))
        x2_s = cross_attend(q_ca[tp:].reshape(nb, s_len, CA_W), cache_mem_k[l].reshape(nb, n_mem, CA_W),
                            cache_mem_v[l].reshape(nb, n_mem, CA_W), x1[tp:].reshape(nb, s_len, d), ca_qn_w[l],
                            wo_ca, tl=s_len)
        x2 = jnp.concatenate([x2_p.reshape(tp, d), x2_s.reshape(ts, d)], axis=0)

        j = l // 2
        if l % 2 == 0:
            x = ffn_dense(x2, norm3_w[l], _w(ffn_w_gate[j], hp), _w(ffn_w_up[j], hp), _w(ffn_w_down[j], hp),
                          ff_block=FF_CHUNK if hp else D_FF)
        else:
            router_pad = jnp.concatenate([moe_router[j], jnp.zeros((d, LANES - N_EXPERTS), F32)], axis=1)
            x = moe_ffn(x2, norm3_w[l], router_pad, moe_w_gate[j].astype(BF16), moe_w_up[j].astype(BF16),
                        moe_w_down[j].astype(BF16))

    st = lambda k: jnp.stack(outs[k])
    return (x[:tp].reshape(bp, seq, d), x[tp:].reshape(nb, s_len, d),
            st(0), st(1), st(2), st(3), st(4), st(5), st(6), st(7), st(8), st(9), st(10), st(11))
```

```python
import functools

import numpy as np
import jax
import jax.numpy as jnp
from jax import lax
from jax.experimental import pallas as pl
from jax.experimental.pallas import tpu as pltpu

F32 = jnp.float32
BF16 = jnp.bfloat16
EPS = 1e-6

D_MODEL = 1024
GROUP_W = 256
POOL_WINDOWS = (2, 4, 8, 16)
POOL_PAD = 15
MLA_HEADS = 4
MLA_NOPE = 64
MLA_ROPE = 32
MLA_V = 64
MLA_Q_LORA = 256
MLA_KV_LORA = 128
MLA_SCALE = (MLA_NOPE + MLA_ROPE) ** -0.5
ROPE_THETA = 10000.0
PAGE_SIZE = 128
HG_HEADS = 4
HG_DK = 128
HG_DV = 64
CONV_W = 31
CONV_PAD = 30
CONV_C = 256
N_MEM = 256
CA_HEADS = 4
CA_HD = 128
CA_W = 512
D_FF = 2816
N_EXPERTS = 8

LANES = 128
VMEM_LIMIT_BYTES = 56 * 1024 * 1024
TOK_TILE = 512
FF_CHUNK = 256
MOE_TILE = 512
HG_CHUNK = 64
HG_SUB = 16
PAGES_PER_STEP = 16


def _cparams(*sem):
    return pltpu.CompilerParams(dimension_semantics=sem, vmem_limit_bytes=VMEM_LIMIT_BYTES)


def _rms(x, w, n=None):
    n = x.shape[-1] if n is None else n
    r = lax.rsqrt(jnp.sum(x * x, axis=-1, keepdims=True) * (1.0 / n) + EPS)
    return x * r * w


def _dot(a, b):
    return jnp.dot(a, b, preferred_element_type=F32)


def _dot_nt(a, b):
    return lax.dot_general(a, b, (((1,), (1,)), ((), ())), preferred_element_type=F32)


def _lane(shape):
    return lax.broadcasted_iota(jnp.int32, shape, len(shape) - 1)


def _row(shape):
    return lax.broadcasted_iota(jnp.int32, shape, len(shape) - 2)


def _split(a):
    hi = a.astype(BF16)
    return hi, (a - hi.astype(F32)).astype(BF16)


def _act(a, hp):
    return _split(a) if hp else (a.astype(BF16),)


def _wts(refs, idx=None):
    return tuple(r[...] if idx is None else r[idx] for r in refs)


def _mm(a, b, nt=False):
    d = _dot_nt if nt else _dot
    out = d(a[0], b[0])
    if len(a) > 1:
        out = out + d(a[1], b[0]) + d(a[0], b[1])
    return out


def _w(w, hp):
    return list(_split(w)) if hp else [w.astype(BF16)]


def _full(a, nargs):
    zeros = (0,) * a.ndim
    if nargs == 1:
        return pl.BlockSpec(a.shape, lambda i: zeros)
    if nargs == 2:
        return pl.BlockSpec(a.shape, lambda i, j: zeros)
    return pl.BlockSpec(a.shape, lambda i, j, k: zeros)


def _norm_matmul_kernel(x_ref, nw_ref, *rest, splits, slot_norm, nw):
    w_refs, rest = rest[:nw], rest[nw:]
    if slot_norm:
        sw_ref, out_refs = rest[0], rest[1:]
    else:
        out_refs = rest
    h = _act(_rms(x_ref[...], nw_ref[...]), nw == 2)
    off = 0
    for idx, (o_ref, n) in enumerate(zip(out_refs, splits)):
        y = _mm(h, _wts(w_refs, (slice(None), slice(off, off + n))))
        if slot_norm and idx == 0:
            y = jnp.concatenate(
                [_rms(y[:, s:s + LANES], sw_ref[...]) for s in range(0, n, LANES)], axis=1)
        o_ref[...] = y
        off += n


def norm_matmul(x, nw, w, splits, slot_norm_w=None, tile=TOK_TILE, name="norm_matmul"):
    t, d = x.shape
    n = w[0].shape[1]
    assert sum(splits) == n and t % tile == 0
    in_specs = [pl.BlockSpec((tile, d), lambda i: (i, 0)), pl.BlockSpec((1, d), lambda i: (0, 0))]
    in_specs += [_full(a, 1) for a in w]
    args = [x, nw.reshape(1, d)] + list(w)
    if slot_norm_w is not None:
        in_specs.append(pl.BlockSpec((1, LANES), lambda i: (0, 0)))
        args.append(slot_norm_w.reshape(1, LANES))
    return pl.pallas_call(
        functools.partial(_norm_matmul_kernel, splits=splits, slot_norm=slot_norm_w is not None, nw=len(w)),
        grid=(t // tile,),
        in_specs=in_specs,
        out_specs=[pl.BlockSpec((tile, s), lambda i: (i, 0)) for s in splits],
        out_shape=[jax.ShapeDtypeStruct((t, s), F32) for s in splits],
        compiler_params=_cparams("parallel"),
        name=name,
    )(*args)


def _pool_kernel(u_ref, sc_ref, *rest, tl, nw):
    w_refs, (o_ref, st_ref, hist_ref) = rest[:nw], rest[nw:]
    i = pl.program_id(1)

    @pl.when(i == 0)
    def _():
        hist_ref[...] = jnp.zeros_like(hist_ref)

    u = u_ref[...]
    ext = jnp.concatenate([hist_ref[...], u], axis=0)
    s2 = ext + pltpu.roll(ext, 1, 0)
    s4 = s2 + pltpu.roll(s2, 2, 0)
    s8 = s4 + pltpu.roll(s4, 4, 0)
    s16 = s8 + pltpu.roll(s8, 8, 0)
    lane = _lane((tl, GROUP_W))
    grp = lane // 64
    pooled = jnp.where(grp == 0, s2[16:], jnp.where(grp == 1, s4[16:], jnp.where(grp == 2, s8[16:], s16[16:])))
    win = jnp.where(grp == 0, 2, jnp.where(grp == 1, 4, jnp.where(grp == 2, 8, 16)))
    pos = i * tl + _row((tl, GROUP_W))
    cnt = jnp.minimum(win, pos + 1).astype(F32)
    d = pooled / cnt - u
    o_ref[...] = _mm(_act(d, nw == 2), _wts(w_refs)) * sc_ref[...]
    hist_ref[...] = ext[tl:]
    st_ref[0] = ext[tl:]


def pool_prompt(z_a, w_bd, scale, n_seq, seq_len, tl=512):
    nt = seq_len // tl
    return pl.pallas_call(
        functools.partial(_pool_kernel, tl=tl, nw=len(w_bd)),
        grid=(n_seq, nt),
        in_specs=[pl.BlockSpec((tl, GROUP_W), lambda s, i: (s * nt + i, 0)),
                  pl.BlockSpec((1, GROUP_W), lambda s, i: (0, 0))] + [_full(a, 2) for a in w_bd],
        out_specs=[pl.BlockSpec((tl, GROUP_W), lambda s, i: (s * nt + i, 0)),
                   pl.BlockSpec((1, 16, GROUP_W), lambda s, i: (s, 0, 0))],
        out_shape=[jax.ShapeDtypeStruct((n_seq * seq_len, GROUP_W), F32),
                   jax.ShapeDtypeStruct((n_seq, 16, GROUP_W), F32)],
        scratch_shapes=[pltpu.VMEM((16, GROUP_W), F32)],
        compiler_params=_cparams("parallel", "arbitrary"),
        name="pool_prompt",
    )(z_a, scale.reshape(1, GROUP_W), *w_bd)


def _conv_kernel(u_ref, cw_ref, cb_ref, lw_ref, lb_ref, pb_ref, *rest, tl, nw):
    w_refs, (o_ref, st_ref, ext_ref) = rest[:nw], rest[nw:]
    i = pl.program_id(1)

    @pl.when(i == 0)
    def _():
        ext_ref[0:32, :] = jnp.zeros((32, CONV_C), F32)

    u = u_ref[...]
    g = u[:, :CONV_C] * jax.nn.sigmoid(u[:, CONV_C:])
    ext_ref[32:, :] = g
    y = jnp.zeros((tl, CONV_C), F32)
    for j in range(CONV_W):
        y = y + ext_ref[2 + j:2 + j + tl, :] * cw_ref[j:j + 1, :]
    y = y + cb_ref[...]
    mu = jnp.mean(y, axis=-1, keepdims=True)
    yc = y - mu
    var = jnp.mean(yc * yc, axis=-1, keepdims=True)
    y = yc * lax.rsqrt(var + EPS) * lw_ref[...] + lb_ref[...]
    o_ref[...] = _mm(_act(jax.nn.silu(y), nw == 2), _wts(w_refs)) + pb_ref[...]
    tail = ext_ref[tl:tl + 32, :]
    st_ref[0] = tail
    ext_ref[0:32, :] = tail


def conv_prompt(z_d, cw, cb, lw, lb, pw, pb, n_seq, seq_len, tl=512):
    nt = seq_len // tl
    vec = lambda: pl.BlockSpec((1, CONV_C), lambda s, i: (0, 0))
    return pl.pallas_call(
        functools.partial(_conv_kernel, tl=tl, nw=len(pw)),
        grid=(n_seq, nt),
        in_specs=[pl.BlockSpec((tl, 2 * CONV_C), lambda s, i: (s * nt + i, 0)),
                  pl.BlockSpec((32, CONV_C), lambda s, i: (0, 0)),
                  vec(), vec(), vec(), vec()] + [_full(a, 2) for a in pw],
        out_specs=[pl.BlockSpec((tl, CONV_C), lambda s, i: (s * nt + i, 0)),
                   pl.BlockSpec((1, 32, CONV_C), lambda s, i: (s, 0, 0))],
        out_shape=[jax.ShapeDtypeStruct((n_seq * seq_len, CONV_C), F32),
                   jax.ShapeDtypeStruct((n_seq, 32, CONV_C), F32)],
        scratch_shapes=[pltpu.VMEM((tl + 32, CONV_C), F32)],
        compiler_params=_cparams("parallel", "arbitrary"),
        name="conv_prompt",
    )(z_d, cw, cb.reshape(1, -1), lw.reshape(1, -1), lb.reshape(1, -1), pb.reshape(1, -1), *pw)


def _split_pair(x):
    low = _lane(x.shape) < 64
    return jnp.where(low, x, 0.0), jnp.where(low, pltpu.roll(x, 64, 1), 0.0)


def _hgrn_chunk(zc, lb, onw, tri, s_ref, hp):
    c = zc.shape[0]
    nk = HG_HEADS * HG_DK
    q = zc[:, 0:nk]
    zf = zc[:, nk:2 * nk]
    v_all = zc[:, 2 * nk:2 * nk + 256]
    g_all = zc[:, 2 * nk + 256:2 * nk + 512]
    log_f = jnp.log(lb + (1.0 - lb) * jax.nn.sigmoid(zf))
    k = (1.0 - lb) * jax.nn.sigmoid(-zf)
    hi = log_f.astype(BF16)
    r1 = log_f - hi.astype(F32)
    mid = r1.astype(BF16)
    lo = (r1 - mid.astype(F32)).astype(BF16)
    b = _dot(tri, hi) + _dot(tri, mid) + _dot(tri, lo)
    b_last = b[c - 1:c, :]
    q_inter = q * jnp.exp(b)
    k_tail = k * jnp.exp(b_last - b)
    decay_last = jnp.exp(b_last)
    n_sub = c // HG_SUB
    rowc = _row((HG_SUB, LANES))
    outs = []
    for p in range(2):
        v_pair = _split_pair(v_all[:, LANES * p:LANES * (p + 1)])
        g_pair = _split_pair(g_all[:, LANES * p:LANES * (p + 1)])
        o_pair = []
        for hh in range(2):
            h = 2 * p + hh
            sl = slice(HG_DK * h, HG_DK * (h + 1))
            vh = v_pair[hh]
            vh_t = _act(vh, hp)
            s_old = s_ref[h]
            o = _mm(_act(q_inter[:, sl], hp), _act(s_old, hp))
            bh, qh, kh = b[:, sl], q[:, sl], k[:, sl]
            parts = []
            for ib in range(n_sub):
                r0 = ib * HG_SUB
                b_i, q_i = bh[r0:r0 + HG_SUB], qh[r0:r0 + HG_SUB]
                acc = o[r0:r0 + HG_SUB]
                if ib > 0:
                    ref_b = bh[r0 - 1:r0, :]
                    qt = q_i * jnp.exp(b_i - ref_b)
                    kt = kh[0:r0] * jnp.exp(ref_b - bh[0:r0])
                    a = _mm(_act(qt, hp), _act(kt, hp), nt=True)
                    acc = acc + _mm(_act(a, hp), tuple(t[0:r0] for t in vh_t))
                for s in range(HG_SUB):
                    r = r0 + s
                    e = jnp.exp(jnp.minimum(b_i - bh[r:r + 1, :], 0.0))
                    pr = jnp.where(rowc >= s, q_i * kh[r:r + 1, :] * e, 0.0)
                    acc = acc + jnp.sum(pr, axis=-1, keepdims=True) * vh[r:r + 1, :]
                parts.append(acc)
            o = jnp.concatenate(parts, axis=0) if n_sub > 1 else parts[0]
            dcol = jnp.broadcast_to(decay_last[:, sl], (LANES, LANES)).T
            s_ref[h] = dcol * s_old + _mm(_act(k_tail[:, sl].T, hp), vh_t)
            o = _rms(o, onw, HG_DV) * jax.nn.silu(g_pair[hh])
            o_pair.append(o)
        outs.append(o_pair[0] + pltpu.roll(o_pair[1], 64, 1))
    return jnp.concatenate(outs, axis=1)


def _hgrn_kernel(z_ref, lb_ref, onw_ref, tri_ref, o_ref, st_ref, s_ref, *, tl, hp):
    i = pl.program_id(1)

    @pl.when(i == 0)
    def _():
        s_ref[...] = jnp.zeros_like(s_ref)

    def body(cidx, carry):
        r0 = pl.multiple_of(cidx * HG_CHUNK, HG_CHUNK)
        zc = z_ref[pl.ds(r0, HG_CHUNK), :]
        o_ref[pl.ds(r0, HG_CHUNK), :] = _hgrn_chunk(zc, lb_ref[...], onw_ref[...], tri_ref[...], s_ref, hp)
        return carry

    lax.fori_loop(0, tl // HG_CHUNK, body, 0)
    st_ref[0] = s_ref[...]


def hgrn_prompt(z_c, lb, onw_slot, n_seq, seq_len, hp, tl=512):
    nt = seq_len // tl
    tri = jnp.tril(jnp.ones((HG_CHUNK, HG_CHUNK), F32)).astype(BF16)
    zw = z_c.shape[1]
    return pl.pallas_call(
        functools.partial(_hgrn_kernel, tl=tl, hp=hp),
        grid=(n_seq, nt),
        in_specs=[pl.BlockSpec((tl, zw), lambda s, i: (s * nt + i, 0)),
                  pl.BlockSpec((1, HG_HEADS * HG_DK), lambda s, i: (0, 0)),
                  pl.BlockSpec((1, LANES), lambda s, i: (0, 0)),
                  pl.BlockSpec((HG_CHUNK, HG_CHUNK), lambda s, i: (0, 0))],
        out_specs=[pl.BlockSpec((tl, GROUP_W), lambda s, i: (s * nt + i, 0)),
                   pl.BlockSpec((1, HG_HEADS, HG_DK, LANES), lambda s, i: (s, 0, 0, 0))],
        out_shape=[jax.ShapeDtypeStruct((n_seq * seq_len, GROUP_W), F32),
                   jax.ShapeDtypeStruct((n_seq, HG_HEADS, HG_DK, LANES), F32)],
        scratch_shapes=[pltpu.VMEM((HG_HEADS, HG_DK, LANES), F32)],
        compiler_params=_cparams("parallel", "arbitrary"),
        name="hgrn_prompt",
    )(z_c, lb.reshape(1, -1), onw_slot.reshape(1, LANES), tri)


def _mla_proj_kernel(z_ref, cs_ref, sn_ref, qnw_ref, qv_ref, qvs_ref, kvnw_ref, kv_ref, krv_ref, krvs_ref,
                     *rest, absorb, nw):
    wq_refs, wkv_refs, rest = rest[:nw], rest[nw:2 * nw], rest[2 * nw:]
    if absorb:
        wabs_ref, q_ref, k_ref, v_ref, ckv_ref, kr_ref, qabs_ref = rest
    else:
        q_ref, k_ref, v_ref, ckv_ref, kr_ref = rest
    hp = nw == 2
    z = z_ref[...]
    cos, sin = cs_ref[...], sn_ref[...]
    tm = z.shape[0]
    lane = _lane((tm, LANES))
    is_rope = lane < MLA_ROPE
    kr_raw = z[:, 384:512]
    kr_sw = z[:, 512:640]
    r_kr = lax.rsqrt(jnp.sum(kr_raw * kr_raw, axis=-1, keepdims=True) * (1.0 / MLA_ROPE) + EPS)
    k_rope = (kr_raw * r_kr * krv_ref[...]) * cos + (kr_sw * r_kr * krvs_ref[...]) * sin
    kr_ref[...] = k_rope
    c_kv = _rms(z[:, 256:384], kvnw_ref[...])
    ckv_ref[...] = c_kv
    kv = _mm(_act(c_kv, hp), _wts(wkv_refs))
    v_ref[...] = kv[:, 512:].astype(v_ref.dtype)
    ks = []
    for h in range(MLA_HEADS):
        kraw = kv[:, LANES * h:LANES * (h + 1)]
        ks.append(_rms(kraw, kv_ref[...], MLA_NOPE) + k_rope)
    k_ref[...] = jnp.concatenate(ks, axis=1).astype(k_ref.dtype)
    c_q = _rms(z[:, 0:256], qnw_ref[...])
    qq = _mm(_act(c_q, hp), _wts(wq_refs))
    qs = []
    for h in range(MLA_HEADS):
        x = qq[:, LANES * h:LANES * (h + 1)]
        xs = qq[:, 512 + LANES * h:512 + LANES * (h + 1)]
        x2 = x * x
        r_rope = lax.rsqrt(jnp.sum(jnp.where(is_rope, x2, 0.0), axis=-1, keepdims=True) * (1.0 / MLA_ROPE) + EPS)
        r_nope = lax.rsqrt(jnp.sum(jnp.where(is_rope, 0.0, x2), axis=-1, keepdims=True) * (1.0 / MLA_NOPE) + EPS)
        y = x * jnp.where(is_rope, r_rope, r_nope) * qv_ref[...]
        ysw = xs * r_rope * qvs_ref[...]
        qs.append((y * cos + ysw * sin) * MLA_SCALE)
    q = jnp.concatenate(qs, axis=1)
    q_ref[...] = q.astype(q_ref.dtype)
    if absorb:
        kvec4 = jnp.concatenate([kv_ref[...]] * MLA_HEADS, axis=1)
        qabs_ref[...] = _dot((q * kvec4).astype(BF16), wabs_ref[...])


def mla_proj(z_b, cos_t, sin_t, pw, row0, n_rows, absorb, hp, tile=TOK_TILE):
    assert row0 % tile == 0 and n_rows % tile == 0
    t0 = row0 // tile
    rows = lambda w: pl.BlockSpec((tile, w), lambda i: (i + t0, 0))
    out_rows = lambda w: pl.BlockSpec((tile, w), lambda i: (i, 0))
    consts = [pw["q_norm_w"], pw["qvec"], pw["qvec_sw"], pw["kv_norm_w"], pw["kvec"], pw["krvec"], pw["krvec_sw"]]
    consts += pw["wq"] + pw["wkv"]
    if absorb:
        consts.append(pw["wabs"])
    qkv_dt = F32 if hp else BF16
    out_shape = [jax.ShapeDtypeStruct((n_rows, 512), qkv_dt)] * 3 + [jax.ShapeDtypeStruct((n_rows, LANES), F32)] * 2
    out_specs = [out_rows(512)] * 3 + [out_rows(LANES)] * 2
    if absorb:
        out_shape.append(jax.ShapeDtypeStruct((n_rows, 512), F32))
        out_specs.append(out_rows(512))
    return pl.pallas_call(
        functools.partial(_mla_proj_kernel, absorb=absorb, nw=len(pw["wq"])),
        grid=(n_rows // tile,),
        in_specs=[rows(640), rows(LANES), rows(LANES)] + [_full(a, 1) for a in consts],
        out_specs=out_specs,
        out_shape=out_shape,
        compiler_params=_cparams("parallel"),
        name="mla_proj_s" if absorb else "mla_proj_p",
    )(z_b, cos_t, sin_t, *consts)


def _flash_kernel(qi_ref, kj_ref, q_ref, k_ref, v_ref, o_ref, m_ref, l_ref, acc_ref, *, tq, hp):
    n = pl.program_id(1)
    qi, kj = qi_ref[n], kj_ref[n]

    @pl.when(kj == 0)
    def _():
        m_ref[...] = jnp.full_like(m_ref, -jnp.inf)
        l_ref[...] = jnp.zeros_like(l_ref)
        acc_ref[...] = jnp.zeros_like(acc_ref)

    def operand(ref, sl):
        x = ref[:, sl]
        return _split(x) if hp else (x,)

    def step(masked):
        for h in range(MLA_HEADS):
            sl = slice(LANES * h, LANES * (h + 1))
            s = _mm(operand(q_ref, sl), operand(k_ref, sl), nt=True)
            if masked:
                s = jnp.where(_row((tq, tq)) >= _lane((tq, tq)), s, -jnp.inf)
            m_prev = m_ref[h]
            m_new = jnp.maximum(m_prev, jnp.max(s, axis=-1, keepdims=True))
            alpha = jnp.exp(m_prev - m_new)
            p = jnp.exp(s - jnp.tile(m_new, (1, tq // LANES)))
            l_ref[h] = alpha * l_ref[h] + jnp.sum(p, axis=-1, keepdims=True)
            acc_ref[h] = alpha * acc_ref[h] + _mm(_act(p, hp), operand(v_ref, sl))
            m_ref[h] = m_new

    @pl.when(kj < qi)
    def _():
        step(False)

    @pl.when(kj == qi)
    def _():
        step(True)
        o_ref[...] = jnp.concatenate([acc_ref[h] / l_ref[h] for h in range(MLA_HEADS)], axis=1)


def flash_prompt(q, k, v, n_seq, seq_len, hp, tq=512):
    nq = seq_len // tq
    qi = np.array([i for i in range(nq) for _ in range(i + 1)], np.int32)
    kj = np.array([j for i in range(nq) for j in range(i + 1)], np.int32)
    grid_spec = pltpu.PrefetchScalarGridSpec(
        num_scalar_prefetch=2,
        grid=(n_seq, len(qi)),
        in_specs=[pl.BlockSpec((tq, 512), lambda b, n, qi, kj: (b * nq + qi[n], 0)),
                  pl.BlockSpec((tq, 512), lambda b, n, qi, kj: (b * nq + kj[n], 0)),
                  pl.BlockSpec((tq, 512), lambda b, n, qi, kj: (b * nq + kj[n], 0))],
        out_specs=pl.BlockSpec((tq, 512), lambda b, n, qi, kj: (b * nq + qi[n], 0)),
        scratch_shapes=[pltpu.VMEM((MLA_HEADS, tq, LANES), F32), pltpu.VMEM((MLA_HEADS, tq, LANES), F32),
                        pltpu.VMEM((MLA_HEADS, tq, LANES), F32)])
    return pl.pallas_call(
        functools.partial(_flash_kernel, tq=tq, hp=hp),
        grid_spec=grid_spec,
        out_shape=jax.ShapeDtypeStruct((n_seq * seq_len, 512), F32),
        compiler_params=_cparams("parallel", "arbitrary"),
        name="mla_flash_prompt",
    )(jnp.asarray(qi), jnp.asarray(kj), q, k, v)


def _mla_sample_kernel(pt_ref, *refs, npg, n_steps):
    c_refs = refs[:npg]
    r_refs = refs[npg:2 * npg]
    qa_ref, qr_ref, cn_ref, rn_ref, wuk_ref, seg_ref, o_ref, m_ref, l_ref, acc_ref = refs[2 * npg:]
    g = pl.program_id(1)

    @pl.when(g == 0)
    def _():
        m_ref[...] = jnp.full_like(m_ref, -jnp.inf)
        l_ref[...] = jnp.zeros_like(l_ref)
        acc_ref[...] = jnp.zeros_like(acc_ref)

    qa = qa_ref[...].astype(BF16)
    qr = qr_ref[...][:, :MLA_ROPE].astype(BF16)
    wuk = wuk_ref[...]
    seg = seg_ref[...]

    def nope_scores(cb):
        kraw = _dot(cb, wuk)
        ssq = _dot_nt(seg, (kraw * kraw).astype(BF16))
        return _dot_nt(qa, cb) * lax.rsqrt(ssq * (1.0 / MLA_NOPE) + EPS)

    def update(s_all, cb):
        m_prev = m_ref[...]
        m_new = jnp.maximum(m_prev, jnp.max(s_all, axis=-1, keepdims=True))
        alpha = jnp.exp(m_prev - m_new)
        p_all = jnp.exp(s_all - m_new)
        l_ref[...] = alpha * l_ref[...] + jnp.sum(p_all, axis=-1, keepdims=True)
        acc_ref[...] = alpha * acc_ref[...] + _dot(p_all.astype(BF16), cb)
        m_ref[...] = m_new

    cb_all = jnp.concatenate([c_refs[p][...].astype(BF16) for p in range(npg)], axis=0)
    kr_all = jnp.concatenate([r_refs[p][...].astype(BF16) for p in range(npg)], axis=1)
    update(nope_scores(cb_all) + _dot(qr, kr_all), cb_all)

    @pl.when(g == n_steps - 1)
    def _():
        cb = cn_ref[...].astype(BF16)
        s_new = nope_scores(cb) + _dot_nt(qr, rn_ref[...][:, :MLA_ROPE].astype(BF16))
        qtok = _row(s_new.shape) // MLA_HEADS
        s_new = jnp.where(_lane(s_new.shape) <= qtok, s_new, -jnp.inf)
        update(s_new, cb)
        o_ref[...] = acc_ref[...] / l_ref[...]


def mla_sample(page_table, cache_c, cache_rt, layer, qa, qr, c_new, r_new, wuk, seg):
    nb, n_pages = page_table.shape
    npg = PAGES_PER_STEP
    n_steps = n_pages // npg
    rows = qa.shape[1]
    s_pad = c_new.shape[1]

    def page_spec(shape, p):
        return pl.BlockSpec((None, None) + shape, lambda b, g, pt: (layer, pt[b * n_pages + g * npg + p], 0, 0))

    per_seq = lambda r, w: pl.BlockSpec((None, r, w), lambda b, g, pt: (b, 0, 0))
    full = lambda a: pl.BlockSpec(a.shape, lambda b, g, pt: (0,) * a.ndim)
    grid_spec = pltpu.PrefetchScalarGridSpec(
        num_scalar_prefetch=1,
        grid=(nb, n_steps),
        in_specs=[page_spec((PAGE_SIZE, MLA_KV_LORA), p) for p in range(npg)]
                 + [page_spec((MLA_ROPE, PAGE_SIZE), p) for p in range(npg)]
                 + [per_seq(rows, LANES), per_seq(rows, LANES), per_seq(s_pad, LANES), per_seq(s_pad, LANES),
                    full(wuk), full(seg)],
        out_specs=per_seq(rows, LANES),
        scratch_shapes=[pltpu.VMEM((rows, 1), F32), pltpu.VMEM((rows, 1), F32), pltpu.VMEM((rows, LANES), F32)])
    return pl.pallas_call(
        functools.partial(_mla_sample_kernel, npg=npg, n_steps=n_steps),
        grid_spec=grid_spec,
        out_shape=jax.ShapeDtypeStruct((nb, rows, LANES), F32),
        compiler_params=_cparams("parallel", "arbitrary"),
        name="mla_sample",
    )(page_table.reshape(-1), *([cache_c] * npg), *([cache_rt] * npg), qa, qr, c_new, r_new, wuk, seg)


def _matmul_kernel(a_ref, w_ref, o_ref):
    o_ref[...] = _dot(a_ref[...].astype(BF16), w_ref[...])


def matmul(a, w, tile=TOK_TILE, name="matmul"):
    t, kdim = a.shape
    n = w.shape[1]
    return pl.pallas_call(
        _matmul_kernel,
        grid=(t // tile,),
        in_specs=[pl.BlockSpec((tile, kdim), lambda i: (i, 0)), pl.BlockSpec((kdim, n), lambda i: (0, 0))],
        out_specs=pl.BlockSpec((tile, n), lambda i: (i, 0)),
        out_shape=jax.ShapeDtypeStruct((t, n), F32),
        compiler_params=_cparams("parallel"),
        name=name,
    )(a, w)


def _mix_out_kernel(oa_ref, ob_ref, oc_ref, od_ref, x_ref, ga_ref, gb_ref, gc_ref, gd_ref, n2_ref, *rest, nw):
    w_groups = [rest[nw * g:nw * (g + 1)] for g in range(5)]
    x1_ref, q_ref = rest[5 * nw:]
    hp = nw == 2
    acc = x_ref[...]
    for o_ref, g_ref, w_refs in ((oa_ref, ga_ref, w_groups[0]), (ob_ref, gb_ref, w_groups[1]),
                                 (oc_ref, gc_ref, w_groups[2]), (od_ref, gd_ref, w_groups[3])):
        acc = acc + _mm(_act(_rms(o_ref[...], g_ref[...], GROUP_W), hp), _wts(w_refs))
    x1_ref[...] = acc
    q_ref[...] = _mm(_act(_rms(acc, n2_ref[...]), hp), _wts(w_groups[4]))


def mix_out(o_a, o_b, o_c, o_d, x, gw, w_out, n2w, wq, tile=TOK_TILE):
    t = x.shape[0]
    rows = lambda a: pl.BlockSpec((tile, a.shape[1]), lambda i: (i, 0))
    acts = [o_a, o_b, o_c, o_d, x]
    consts = list(gw) + [n2w.reshape(1, -1)] + [a for w in w_out for a in w] + list(wq)
    return pl.pallas_call(
        functools.partial(_mix_out_kernel, nw=len(wq)),
        grid=(t // tile,),
        in_specs=[rows(a) for a in acts] + [_full(a, 1) for a in consts],
        out_specs=[pl.BlockSpec((tile, D_MODEL), lambda i: (i, 0)), pl.BlockSpec((tile, CA_W), lambda i: (i, 0))],
        out_shape=[jax.ShapeDtypeStruct((t, D_MODEL), F32), jax.ShapeDtypeStruct((t, CA_W), F32)],
        compiler_params=_cparams("parallel"),
        name="mix_out",
    )(*acts, *consts)


def _cross_kernel(q_ref, mk_ref, mv_ref, x_ref, qn_ref, *rest, nw):
    wo_refs, (o_ref,) = rest[:nw], rest[nw:]
    hp = nw == 2
    q = q_ref[...]
    outs = []
    for h in range(CA_HEADS):
        sl = slice(CA_HD * h, CA_HD * (h + 1))
        qh = _rms(q[:, sl], qn_ref[...])
        s = _mm(_act(qh, hp), _act(mk_ref[:, sl], hp), nt=True) * (CA_HD ** -0.5)
        s = s - jnp.max(s, axis=-1, keepdims=True)
        e = jnp.exp(s)
        p = e / jnp.sum(e, axis=-1, keepdims=True)
        outs.append(_mm(_act(p, hp), _act(mv_ref[:, sl], hp)))
    o = jnp.concatenate(outs, axis=1)
    o_ref[...] = x_ref[...] + _mm(_act(o, hp), _wts(wo_refs))


def cross_attend(q, mem_k, mem_v, x, qnw, wo, tl):
    n_seq, seq_len, _ = q.shape
    nt = seq_len // tl
    return pl.pallas_call(
        functools.partial(_cross_kernel, nw=len(wo)),
        grid=(n_seq, nt),
        in_specs=[pl.BlockSpec((None, tl, CA_W), lambda s, i: (s, i, 0)),
                  pl.BlockSpec((None, N_MEM, CA_W), lambda s, i: (s, 0, 0)),
                  pl.BlockSpec((None, N_MEM, CA_W), lambda s, i: (s, 0, 0)),
                  pl.BlockSpec((None, tl, D_MODEL), lambda s, i: (s, i, 0)),
                  pl.BlockSpec((1, CA_HD), lambda s, i: (0, 0))] + [_full(a, 2) for a in wo],
        out_specs=pl.BlockSpec((None, tl, D_MODEL), lambda s, i: (s, i, 0)),
        out_shape=jax.ShapeDtypeStruct((n_seq, seq_len, D_MODEL), F32),
        compiler_params=_cparams("parallel", "parallel"),
        name="cross_attend",
    )(q, mem_k, mem_v, x, qnw.reshape(1, CA_HD), *wo)


def _swiglu_into(h, wg_refs, wu_refs, wd_refs, o_ref, width):
    hp = len(wg_refs) == 2
    for c0 in range(0, width, FF_CHUNK):
        cols = (slice(None), slice(c0, c0 + FF_CHUNK))
        g = _mm(h, _wts(wg_refs, cols))
        u = _mm(h, _wts(wu_refs, cols))
        a = _act(jax.nn.silu(g) * u, hp)
        o_ref[...] += _mm(a, _wts(wd_refs, (slice(c0, c0 + FF_CHUNK), slice(None))))


def _ffn_kernel(x_ref, nw_ref, *rest, nw, width):
    wg, wu, wd, (o_ref,) = rest[:nw], rest[nw:2 * nw], rest[2 * nw:3 * nw], rest[3 * nw:]
    x = x_ref[...]

    @pl.when(pl.program_id(1) == 0)
    def _():
        o_ref[...] = x

    _swiglu_into(_act(_rms(x, nw_ref[...]), nw == 2), wg, wu, wd, o_ref, width)


def ffn_dense(x, nw, wg, wu, wd, ff_block, tile=TOK_TILE):
    t = x.shape[0]
    up = lambda: pl.BlockSpec((D_MODEL, ff_block), lambda i, j: (0, j))
    down = lambda: pl.BlockSpec((ff_block, D_MODEL), lambda i, j: (j, 0))
    n = len(wg)
    return pl.pallas_call(
        functools.partial(_ffn_kernel, nw=n, width=ff_block),
        grid=(t // tile, D_FF // ff_block),
        in_specs=[pl.BlockSpec((tile, D_MODEL), lambda i, j: (i, 0)), pl.BlockSpec((1, D_MODEL), lambda i, j: (0, 0))]
                 + [up() for _ in range(2 * n)] + [down() for _ in range(n)],
        out_specs=pl.BlockSpec((tile, D_MODEL), lambda i, j: (i, 0)),
        out_shape=jax.ShapeDtypeStruct((t, D_MODEL), F32),
        compiler_params=_cparams("parallel", "arbitrary"),
        name="ffn_dense",
    )(x, nw.reshape(1, -1), *wg, *wu, *wd)


def _router_kernel(x_ref, nw_ref, r_ref, h_ref, g_ref):
    h = _rms(x_ref[...], nw_ref[...])
    h_ref[...] = h
    logits = jnp.dot(h, r_ref[...], preferred_element_type=F32, precision=lax.Precision.HIGHEST)
    lane = _lane(logits.shape)
    neg = -jnp.inf
    l1 = jnp.where(lane < N_EXPERTS, logits, neg)
    m1 = jnp.max(l1, axis=-1, keepdims=True)
    i1 = jnp.min(jnp.where(l1 == m1, lane, LANES), axis=-1, keepdims=True)
    l2 = jnp.where(lane == i1, neg, l1)
    m2 = jnp.max(l2, axis=-1, keepdims=True)
    i2 = jnp.min(jnp.where(l2 == m2, lane, LANES), axis=-1, keepdims=True)
    e2 = jnp.exp(m2 - m1)
    den = 1.0 + e2
    gates = jnp.where(lane == i1, 1.0 / den, jnp.where(lane == i2, e2 / den, 0.0))
    g_ref[...] = jnp.where(lane == N_EXPERTS, i1.astype(F32), jnp.where(lane == N_EXPERTS + 1, i2.astype(F32), gates))


def moe_router(x, nw, router_pad, tile=TOK_TILE):
    t = x.shape[0]
    return pl.pallas_call(
        _router_kernel,
        grid=(t // tile,),
        in_specs=[pl.BlockSpec((tile, D_MODEL), lambda i: (i, 0)), pl.BlockSpec((1, D_MODEL), lambda i: (0, 0)),
                  pl.BlockSpec((D_MODEL, LANES), lambda i: (0, 0))],
        out_specs=[pl.BlockSpec((tile, D_MODEL), lambda i: (i, 0)), pl.BlockSpec((tile, LANES), lambda i: (i, 0))],
        out_shape=[jax.ShapeDtypeStruct((t, D_MODEL), F32), jax.ShapeDtypeStruct((t, LANES), F32)],
        compiler_params=_cparams("parallel"),
        name="moe_router",
    )(x, nw.reshape(1, -1), router_pad)


def _experts_kernel(be_ref, nv_ref, src_ref, nxt_ref, h_hbm, wg_ref, wu_ref, wd_ref, o_ref, xbuf, sem, *, tile):
    b = pl.program_id(0)
    slot = b % 2
    nv = nv_ref[0]

    def row_copy(idx_ref, i, s):
        return pltpu.make_async_copy(h_hbm.at[pl.ds(idx_ref[0, i], 1), :], xbuf.at[s, pl.ds(i, 1), :], sem.at[s])

    def start_rows(idx_ref, s):
        def body(i, c):
            row_copy(idx_ref, i, s).start()
            return c
        lax.fori_loop(0, tile, body, 0, unroll=8)

    @pl.when((b == 0) & (nv > 0))
    def _():
        start_rows(src_ref, 0)

    @pl.when(b + 1 < nv)
    def _():
        start_rows(nxt_ref, 1 - slot)

    o_ref[...] = jnp.zeros_like(o_ref)

    @pl.when(b < nv)
    def _():
        def body(i, c):
            row_copy(src_ref, i, slot).wait()
            return c
        lax.fori_loop(0, tile, body, 0, unroll=8)
        _swiglu_into((xbuf[slot].astype(BF16),), (wg_ref,), (wu_ref,), (wd_ref,), o_ref, D_FF)


def moe_experts(h, src, blk_e, n_valid, wg, wu, wd, j, tile=MOE_TILE):
    n_blk = src.shape[0]
    wspec = lambda a: pl.BlockSpec((None, None) + a.shape[2:], lambda b, be, nv: (j, be[b], 0, 0))
    idx_spec = lambda f: pl.BlockSpec((None, 1, tile), f, memory_space=pltpu.SMEM)
    grid_spec = pltpu.PrefetchScalarGridSpec(
        num_scalar_prefetch=2,
        grid=(n_blk,),
        in_specs=[idx_spec(lambda b, be, nv: (b, 0, 0)),
                  idx_spec(lambda b, be, nv: (jnp.minimum(b + 1, n_blk - 1), 0, 0)),
                  pl.BlockSpec(memory_space=pl.ANY), wspec(wg), wspec(wu), wspec(wd)],
        out_specs=pl.BlockSpec((tile, D_MODEL), lambda b, be, nv: (b, 0)),
        scratch_shapes=[pltpu.VMEM((2, tile, D_MODEL), F32), pltpu.SemaphoreType.DMA((2,))])
    return pl.pallas_call(
        functools.partial(_experts_kernel, tile=tile),
        grid_spec=grid_spec,
        out_shape=jax.ShapeDtypeStruct((n_blk * tile, D_MODEL), F32),
        compiler_params=_cparams("arbitrary"),
        name="moe_experts",
    )(blk_e, n_valid, src, src, h, wg, wu, wd)


def moe_ffn(x, nw, router_pad, wg, wu, wd, j):
    t = x.shape[0]
    h, gfull = moe_router(x, nw, router_pad)
    e_idx = gfull[:, N_EXPERTS:N_EXPERTS + 2].astype(jnp.int32)
    g2 = jnp.take_along_axis(gfull[:, :N_EXPERTS], e_idx, axis=1)
    sel = (jnp.arange(N_EXPERTS, dtype=jnp.int32)[None, :] == e_idx[:, 0:1]) | (
        jnp.arange(N_EXPERTS, dtype=jnp.int32)[None, :] == e_idx[:, 1:2])
    pos = jnp.cumsum(sel.astype(jnp.int32), axis=0) - 1
    counts = pos[-1] + 1
    nb_e = (counts + MOE_TILE - 1) // MOE_TILE
    blk_end = jnp.cumsum(nb_e)
    row_start = (blk_end - nb_e) * MOE_TILE
    slot = row_start[None, :] + pos
    assert (2 * t) % MOE_TILE == 0
    n_blk = (2 * t + N_EXPERTS * MOE_TILE) // MOE_TILE
    stride = t + MOE_TILE
    big = N_EXPERTS * stride
    tok = jnp.arange(t, dtype=jnp.int32)[:, None]
    real = e_idx * stride + tok
    pad_e = nb_e * MOE_TILE - counts
    fill = jnp.arange(MOE_TILE, dtype=jnp.int32)[None, :]
    dummy = jnp.where(fill < pad_e[:, None],
                      jnp.arange(N_EXPERTS, dtype=jnp.int32)[:, None] * stride + t + fill, big)
    keys = jnp.sort(jnp.concatenate([real.reshape(-1), dummy.reshape(-1)]))
    src = keys % stride
    src = jnp.where((src >= t) | (keys >= big), 0, src).reshape(n_blk, 1, MOE_TILE)
    blk_e = jnp.minimum(jnp.searchsorted(blk_end, jnp.arange(n_blk, dtype=jnp.int32), side="right"),
                        N_EXPERTS - 1).astype(jnp.int32)
    y_buf = moe_experts(h, src, blk_e, blk_end[-1:].astype(jnp.int32), wg, wu, wd, j)
    s2 = jnp.take_along_axis(slot, e_idx, axis=1)
    y = y_buf[s2[:, 0]] * g2[:, 0:1] + y_buf[s2[:, 1]] * g2[:, 1:2]
    return x + y


def _zeros(r, c):
    return jnp.zeros((r, c), F32)


def _prep_w_in(w, hp):
    d = w.shape[0]
    k_r = w[:, 640:672]
    kr_slot = jnp.concatenate([k_r, _zeros(d, 96)], axis=1)
    kr_sw = jnp.concatenate([k_r[:, 16:], k_r[:, :16], _zeros(d, 96)], axis=1)
    return _w(jnp.concatenate([w[:, 0:256], w[:, 256:640], kr_slot, kr_sw, w[:, 672:2208], w[:, 2208:2720]],
                              axis=1), hp)


W_IN_SPLITS = (256, 640, 1536, 512)


def _slot_vec(rope_w, nope_w):
    z = jnp.zeros((32,), F32)
    return jnp.concatenate([rope_w, nope_w, z]).reshape(1, LANES)


def _prep_mla(lw, hp):
    wq, wkv = lw["mla_w_uq"], lw["mla_w_ukv"]
    slots, sw = [], []
    for h in range(MLA_HEADS):
        nope = wq[:, 96 * h:96 * h + 64]
        rope = wq[:, 96 * h + 64:96 * h + 96]
        slots.append(jnp.concatenate([rope, nope, _zeros(MLA_Q_LORA, 32)], axis=1))
        sw.append(jnp.concatenate([rope[:, 16:], rope[:, :16], _zeros(MLA_Q_LORA, 96)], axis=1))
    kslots = [jnp.concatenate([_zeros(128, 32), wkv[:, 128 * h:128 * h + 64], _zeros(128, 32)], axis=1)
              for h in range(MLA_HEADS)]
    vslots = [jnp.concatenate([wkv[:, 128 * h + 64:128 * h + 128], _zeros(128, 64)], axis=1)
              for h in range(MLA_HEADS)]
    z96 = jnp.zeros((96,), F32)
    qr, kr = lw["mla_qn_rope_w"], lw["mla_kn_rope_w"]
    wabs = jnp.zeros((512, 512), F32)
    for h in range(MLA_HEADS):
        wabs = wabs.at[128 * h + 32:128 * h + 96, 128 * h:128 * h + 128].set(wkv[:, 128 * h:128 * h + 64].T)
    wuv_bd = jnp.zeros((512, 256), F32)
    for h in range(MLA_HEADS):
        wuv_bd = wuv_bd.at[128 * h:128 * h + 128, 64 * h:64 * h + 64].set(wkv[:, 128 * h + 64:128 * h + 128])
    return dict(
        q_norm_w=lw["mla_q_norm_w"].reshape(1, -1),
        wq=_w(jnp.concatenate(slots + sw, axis=1), hp),
        qvec=_slot_vec(qr, lw["mla_qn_nope_w"]),
        qvec_sw=jnp.concatenate([qr[16:], qr[:16], z96]).reshape(1, LANES),
        kv_norm_w=lw["mla_kv_norm_w"].reshape(1, -1),
        wkv=_w(jnp.concatenate(kslots + vslots, axis=1), hp),
        kvec=_slot_vec(jnp.zeros((32,), F32), lw["mla_kn_nope_w"]),
        krvec=jnp.concatenate([kr, z96]).reshape(1, LANES),
        krvec_sw=jnp.concatenate([kr[16:], kr[:16], z96]).reshape(1, LANES),
        wabs=wabs.astype(BF16),
        wuk=jnp.concatenate([wkv[:, 128 * h:128 * h + 64] for h in range(MLA_HEADS)], axis=1).astype(BF16),
        wuv_bd=wuv_bd.astype(BF16),
    )


def _rope_tables(pos):
    half = MLA_ROPE // 2
    inv = 1.0 / (ROPE_THETA ** (jnp.arange(half, dtype=F32) / half))
    ang = pos.astype(F32)[:, None] * inv[None, :]
    cos, sin = jnp.cos(ang), jnp.sin(ang)
    n = pos.shape[0]
    cos_t = jnp.concatenate([cos, cos, jnp.ones((n, 64), F32), jnp.zeros((n, 32), F32)], axis=1)
    sin_t = jnp.concatenate([-sin, sin, jnp.zeros((n, 96), F32)], axis=1)
    return cos_t, sin_t


def _pad_new(a):
    n, s, w = a.shape
    return jnp.concatenate([a, jnp.zeros((n, 8 - s, w), a.dtype)], axis=1)


def _pad_rows(w, rows_per, pad_to):
    g = w.shape[0] // rows_per
    w = w.reshape(g, rows_per, -1)
    return jnp.concatenate([w, jnp.zeros((g, pad_to - rows_per, w.shape[-1]), w.dtype)], axis=1).reshape(
        g * pad_to, -1)


def _sample_pool(u, prefix, w_bd, scale):
    n, L, c = u.shape
    ext = jnp.concatenate([prefix, u], axis=1)
    cs = jnp.concatenate([jnp.zeros((n, 1, c), F32), jnp.cumsum(ext, axis=1)], axis=1)
    hi = cs[:, POOL_PAD + 1:]
    pooled = []
    for g, w in enumerate(POOL_WINDOWS):
        sl = slice(64 * g, 64 * (g + 1))
        lo = cs[:, POOL_PAD + 1 - w:POOL_PAD + 1 - w + L, sl]
        pooled.append((hi[..., sl] - lo) / float(w))
    d = jnp.concatenate(pooled, axis=-1) - u
    y = jnp.dot(d.reshape(n * L, c).astype(BF16), w_bd, preferred_element_type=F32) * scale
    return y, ext[:, -POOL_PAD:]


def _sample_hgrn(zc, s0, lb, onw):
    n, L, _ = zc.shape
    q = zc[..., 0:512].reshape(n, L, HG_HEADS, HG_DK)
    zf = zc[..., 512:1024].reshape(n, L, HG_HEADS, HG_DK)
    v = zc[..., 1024:1280].reshape(n, L, HG_HEADS, HG_DV)
    g = zc[..., 1280:1536].reshape(n, L, HG_HEADS, HG_DV)
    lbh = lb.reshape(HG_HEADS, HG_DK)
    f = lbh + (1.0 - lbh) * jax.nn.sigmoid(zf)
    k = (1.0 - lbh) * jax.nn.sigmoid(-zf)
    s = s0
    outs = []
    for t in range(L):
        s = f[:, t][..., None] * s + k[:, t][..., None] * v[:, t][:, :, None, :]
        outs.append(jnp.sum(s * q[:, t][..., None], axis=2))
    o = jnp.stack(outs, axis=1)
    o = _rms(o, onw, HG_DV) * jax.nn.silu(g)
    return o.reshape(n * L, GROUP_W), s


def _sample_conv(u, prefix, cw, cb, lw, lb, pw, pb):
    n, L, _ = u.shape
    g = u[..., :CONV_C] * jax.nn.sigmoid(u[..., CONV_C:])
    ext = jnp.concatenate([prefix, g], axis=1)
    y = sum(ext[:, j:j + L, :] * cw[j][None, None, :] for j in range(CONV_W)) + cb
    mu = jnp.mean(y, axis=-1, keepdims=True)
    yc = y - mu
    var = jnp.mean(yc * yc, axis=-1, keepdims=True)
    y = yc * lax.rsqrt(var + EPS) * lw + lb
    y = jnp.dot(jax.nn.silu(y).reshape(n * L, CONV_C).astype(BF16), pw, preferred_element_type=F32) + pb
    return y, ext[:, -CONV_PAD:]


def kernel(x_prompt, x_sample, cache_kv_latent, cache_k_rope, cache_mem_k, cache_mem_v, state_pool, state_hgrn, state_conv, page_table, mem_prompt, norm1_w, w_in, pool_w, pool_scale, mla_q_norm_w, mla_w_uq, mla_kv_norm_w, mla_w_ukv, mla_qn_nope_w, mla_qn_rope_w, mla_kn_nope_w, mla_kn_rope_w, hg_lb_param, hg_onorm_w, conv_w, conv_b, conv_ln_w, conv_ln_b, conv_pw_w, conv_pw_b, grp_norm_w, w_out, norm2_w, mem_norm_w, ca_wq, ca_wk, ca_wv, ca_qn_w, ca_kn_w, ca_wo, norm3_w, ffn_w_gate, ffn_w_up, ffn_w_down, moe_router, moe_w_gate, moe_w_up, moe_w_down):
    bp, seq, d = x_prompt.shape
    nb, s_len, _ = x_sample.shape
    depth = w_in.shape[0]
    tp, ts = bp * seq, nb * s_len
    n_past = page_table.shape[1] * PAGE_SIZE
    n_mem = mem_prompt.shape[1]
    first_moe = 1

    sm = jax.nn.softmax(hg_lb_param.astype(F32), axis=0)
    lower_bounds = jnp.cumsum(sm, axis=0) - sm[:1]

    pos = jnp.concatenate([jnp.tile(jnp.arange(seq, dtype=jnp.int32), bp),
                           jnp.tile(n_past + jnp.arange(s_len, dtype=jnp.int32), nb)])
    cos_t, sin_t = _rope_tables(pos)
    seg = (jnp.arange(256)[None, :] // 64 == jnp.arange(MLA_HEADS * s_len)[:, None] % MLA_HEADS).astype(BF16)
    cache_rt = jnp.swapaxes(cache_k_rope, 2, 3)
    moe_wg, moe_wu, moe_wd = moe_w_gate.astype(BF16), moe_w_up.astype(BF16), moe_w_down.astype(BF16)

    x = jnp.concatenate([x_prompt.reshape(tp, d), x_sample.reshape(ts, d)], axis=0)
    mem2d = mem_prompt.reshape(bp * n_mem, d)
    outs = [[] for _ in range(12)]

    for l in range(depth):
        hp = l <= first_moe
        lw = dict(mla_q_norm_w=mla_q_norm_w[l], mla_w_uq=mla_w_uq[l], mla_kv_norm_w=mla_kv_norm_w[l],
                  mla_w_ukv=mla_w_ukv[l], mla_qn_nope_w=mla_qn_nope_w[l], mla_qn_rope_w=mla_qn_rope_w[l],
                  mla_kn_nope_w=mla_kn_nope_w[l], mla_kn_rope_w=mla_kn_rope_w[l])
        pw = _prep_mla(lw, hp)
        z_a, z_b, z_c, z_d = norm_matmul(x, norm1_w[l], _prep_w_in(w_in[l], hp), W_IN_SPLITS, name="norm_w_in")

        w_bd = _w(jax.scipy.linalg.block_diag(*[pool_w[l, g] for g in range(4)]), hp)
        o_a_p, pool_st = pool_prompt(z_a, w_bd, pool_scale[l], bp, seq)
        o_a_s, pool_st_s = _sample_pool(z_a[tp:].reshape(nb, s_len, GROUP_W), state_pool[l], w_bd[0], pool_scale[l])
        outs[0].append(pool_st[:, 1:])
        outs[7].append(pool_st_s)

        q_p, k_p, v_p, ckv_p, kr_p = mla_proj(z_b, cos_t, sin_t, pw, 0, tp, absorb=False, hp=hp)
        o_b_p = flash_prompt(q_p, k_p, v_p, bp, seq, hp)
        q_s, _, _, ckv_s, kr_s, qabs_s = mla_proj(z_b, cos_t, sin_t, pw, tp, ts, absorb=True, hp=hp)
        rows = MLA_HEADS * s_len
        o_lat = mla_sample(page_table, cache_kv_latent, cache_rt, l,
                           qabs_s.reshape(nb, rows, LANES), q_s.astype(F32).reshape(nb, rows, LANES),
                           _pad_new(ckv_s.reshape(nb, s_len, LANES)), _pad_new(kr_s.reshape(nb, s_len, LANES)),
                           pw["wuk"], seg)
        o_b_s = matmul(o_lat.reshape(ts, 512), pw["wuv_bd"], name="mla_v_up")
        o_b_s = jnp.concatenate([o_b_s.reshape(ts, 4, 64), jnp.zeros((ts, 4, 64), F32)], axis=-1).reshape(ts, 512)
        outs[1].append(ckv_p.reshape(bp, seq, MLA_KV_LORA))
        outs[2].append(kr_p[:, :MLA_ROPE].reshape(bp, seq, MLA_ROPE))
        outs[8].append(ckv_s.reshape(nb, s_len, MLA_KV_LORA))
        outs[9].append(kr_s[:, :MLA_ROPE].reshape(nb, s_len, MLA_ROPE))

        onw_slot = jnp.concatenate([hg_onorm_w[l], jnp.zeros((64,), F32)])
        o_c_p, hg_st = hgrn_prompt(z_c, lower_bounds[l], onw_slot, bp, seq, hp)
        o_c_s, hg_st_s = _sample_hgrn(z_c[tp:].reshape(nb, s_len, -1), state_hgrn[l], lower_bounds[l], hg_onorm_w[l])
        outs[3].append(hg_st[..., :HG_DV])
        outs[10].append(hg_st_s)

        cw_pad = jnp.concatenate([conv_w[l], jnp.zeros((1, CONV_C), F32)], axis=0)
        pw_c = _w(conv_pw_w[l], hp)
        o_d_p, conv_st = conv_prompt(z_d, cw_pad, conv_b[l], conv_ln_w[l], conv_ln_b[l], pw_c, conv_pw_b[l], bp, seq)
        o_d_s, conv_st_s = _sample_conv(z_d[tp:].reshape(nb, s_len, -1), state_conv[l], conv_w[l], conv_b[l],
                                        conv_ln_w[l], conv_ln_b[l], pw_c[0], conv_pw_b[l])
        outs[4].append(conv_st[:, 2:])
        outs[11].append(conv_st_s)

        o_a = jnp.concatenate([o_a_p, o_a_s], axis=0)
        o_b = jnp.concatenate([o_b_p, o_b_s], axis=0)
        o_c = jnp.concatenate([o_c_p, o_c_s], axis=0)
        o_d = jnp.concatenate([o_d_p, o_d_s], axis=0)
        gnw = grp_norm_w[l]
        gw = [gnw[0:256].reshape(1, -1), _pad_rows(gnw[256:512].reshape(-1, 1), 64, 128).reshape(1, -1),
              gnw[512:768].reshape(1, -1), gnw[768:1024].reshape(1, -1)]
        wo_l = w_out[l]
        w_out_parts = [_w(wo_l[0:256], hp), _w(_pad_rows(wo_l[256:512], 64, 128), hp),
                       _w(wo_l[512:768], hp), _w(wo_l[768:1024], hp)]
        x1, q_ca = mix_out(o_a, o_b, o_c, o_d, x, gw, w_out_parts, norm2_w[l], _w(ca_wq[l], hp))

        w_kv = _w(jnp.concatenate([ca_wk[l], ca_wv[l]], axis=1), hp)
        mk_p, mv_p = norm_matmul(mem2d, mem_norm_w[l], w_kv, (CA_W, CA_W), slot_norm_w=ca_kn_w[l],
                                 tile=min(TOK_TILE, mem2d.shape[0]), name="memory_kv")
        outs[5].append(mk_p.reshape(bp, n_mem, CA_HEADS, CA_HD))
        outs[6].append(mv_p.reshape(bp, n_mem, CA_HEADS, CA_HD))

        wo_ca = _w(ca_wo[l], hp)
        x2_p = cross_attend(q_ca[:tp].reshape(bp, seq, CA_W), mk_p.reshape(bp, n_mem, CA_W),
                            mv_p.reshape(bp, n_mem, CA_W), x1[:tp].reshape(bp, seq, d), ca_qn_w[l], wo_ca,
                            tl=min(512, seq))
        x2_s = cross_attend(q_ca[tp:].reshape(nb, s_len, CA_W), cache_mem_k[l].reshape(nb, n_mem, CA_W),
                            cache_mem_v[l].reshape(nb, n_mem, CA_W), x1[tp:].reshape(nb, s_len, d), ca_qn_w[l],
                            wo_ca, tl=s_len)
        x2 = jnp.concatenate([x2_p.reshape(tp, d), x2_s.reshape(ts, d)], axis=0)

        j = l // 2
        if l % 2 == 0:
            x = ffn_dense(x2, norm3_w[l], _w(ffn_w_gate[j], hp), _w(ffn_w_up[j], hp), _w(ffn_w_down[j], hp),
                          ff_block=FF_CHUNK if hp else D_FF)
        else:
            router_pad = jnp.concatenate([moe_router[j], jnp.zeros((d, LANES - N_EXPERTS), F32)], axis=1)
            x = moe_ffn(x2, norm3_w[l], router_pad, moe_wg, moe_wu, moe_wd, j)

    st = lambda k: jnp.stack(outs[k])
    return (x[:tp].reshape(bp, seq, d), x[tp:].reshape(nb, s_len, d),
            st(0), st(1), st(2), st(3), st(4), st(5), st(6), st(7), st(8), st(9), st(10), st(11))
```

```python
import functools

import numpy as np
import jax
import jax.numpy as jnp
from jax import lax
from jax.experimental import pallas as pl
from jax.experimental.pallas import tpu as pltpu

F32 = jnp.float32
BF16 = jnp.bfloat16
EPS = 1e-6

D_MODEL = 1024
GROUP_W = 256
POOL_WINDOWS = (2, 4, 8, 16)
POOL_PAD = 15
MLA_HEADS = 4
MLA_NOPE = 64
MLA_ROPE = 32
MLA_V = 64
MLA_Q_LORA = 256
MLA_KV_LORA = 128
MLA_SCALE = (MLA_NOPE + MLA_ROPE) ** -0.5
ROPE_THETA = 10000.0
PAGE_SIZE = 128
HG_HEADS = 4
HG_DK = 128
HG_DV = 64
CONV_W = 31
CONV_PAD = 30
CONV_C = 256
N_MEM = 256
CA_HEADS = 4
CA_HD = 128
CA_W = 512
D_FF = 2816
N_EXPERTS = 8

LANES = 128
VMEM_LIMIT_BYTES = 56 * 1024 * 1024
TOK_TILE = 512
FF_CHUNK = 256
MOE_TILE = 512
HG_CHUNK = 64
HG_SUB = 16
PAGES_PER_STEP = 16


def _cparams(*sem):
    return pltpu.CompilerParams(dimension_semantics=sem, vmem_limit_bytes=VMEM_LIMIT_BYTES)


def _rms(x, w, n=None):
    n = x.shape[-1] if n is None else n
    r = lax.rsqrt(jnp.sum(x * x, axis=-1, keepdims=True) * (1.0 / n) + EPS)
    return x * r * w


def _dot(a, b):
    return jnp.dot(a, b, preferred_element_type=F32)


def _dot_nt(a, b):
    return lax.dot_general(a, b, (((1,), (1,)), ((), ())), preferred_element_type=F32)


def _lane(shape):
    return lax.broadcasted_iota(jnp.int32, shape, len(shape) - 1)


def _row(shape):
    return lax.broadcasted_iota(jnp.int32, shape, len(shape) - 2)


def _split(a):
    hi = a.astype(BF16)
    return hi, (a - hi.astype(F32)).astype(BF16)


def _act(a, hp):
    return _split(a) if hp else (a.astype(BF16),)


def _wts(refs, idx=None):
    return tuple(r[...] if idx is None else r[idx] for r in refs)


def _mm(a, b, nt=False):
    d = _dot_nt if nt else _dot
    out = d(a[0], b[0])
    if len(a) > 1:
        out = out + d(a[1], b[0]) + d(a[0], b[1])
    return out


def _w(w, hp):
    return list(_split(w)) if hp else [w.astype(BF16)]


def _full(a, nargs):
    zeros = (0,) * a.ndim
    if nargs == 1:
        return pl.BlockSpec(a.shape, lambda i: zeros)
    if nargs == 2:
        return pl.BlockSpec(a.shape, lambda i, j: zeros)
    return pl.BlockSpec(a.shape, lambda i, j, k: zeros)


def _norm_matmul_kernel(x_ref, nw_ref, *rest, splits, slot_norm, nw):
    w_refs, rest = rest[:nw], rest[nw:]
    if slot_norm:
        sw_ref, out_refs = rest[0], rest[1:]
    else:
        out_refs = rest
    h = _act(_rms(x_ref[...], nw_ref[...]), nw == 2)
    off = 0
    for idx, (o_ref, n) in enumerate(zip(out_refs, splits)):
        y = _mm(h, _wts(w_refs, (slice(None), slice(off, off + n))))
        if slot_norm and idx == 0:
            y = jnp.concatenate(
                [_rms(y[:, s:s + LANES], sw_ref[...]) for s in range(0, n, LANES)], axis=1)
        o_ref[...] = y
        off += n


def norm_matmul(x, nw, w, splits, slot_norm_w=None, tile=TOK_TILE, name="norm_matmul"):
    t, d = x.shape
    n = w[0].shape[1]
    assert sum(splits) == n and t % tile == 0
    in_specs = [pl.BlockSpec((tile, d), lambda i: (i, 0)), pl.BlockSpec((1, d), lambda i: (0, 0))]
    in_specs += [_full(a, 1) for a in w]
    args = [x, nw.reshape(1, d)] + list(w)
    if slot_norm_w is not None:
        in_specs.append(pl.BlockSpec((1, LANES), lambda i: (0, 0)))
        args.append(slot_norm_w.reshape(1, LANES))
    return pl.pallas_call(
        functools.partial(_norm_matmul_kernel, splits=splits, slot_norm=slot_norm_w is not None, nw=len(w)),
        grid=(t // tile,),
        in_specs=in_specs,
        out_specs=[pl.BlockSpec((tile, s), lambda i: (i, 0)) for s in splits],
        out_shape=[jax.ShapeDtypeStruct((t, s), F32) for s in splits],
        compiler_params=_cparams("parallel"),
        name=name,
    )(*args)


def _pool_kernel(u_ref, sc_ref, *rest, tl, nw):
    w_refs, (o_ref, st_ref, hist_ref) = rest[:nw], rest[nw:]
    i = pl.program_id(1)

    @pl.when(i == 0)
    def _():
        hist_ref[...] = jnp.zeros_like(hist_ref)

    u = u_ref[...]
    ext = jnp.concatenate([hist_ref[...], u], axis=0)
    s2 = ext + pltpu.roll(ext, 1, 0)
    s4 = s2 + pltpu.roll(s2, 2, 0)
    s8 = s4 + pltpu.roll(s4, 4, 0)
    s16 = s8 + pltpu.roll(s8, 8, 0)
    lane = _lane((tl, GROUP_W))
    grp = lane // 64
    pooled = jnp.where(grp == 0, s2[16:], jnp.where(grp == 1, s4[16:], jnp.where(grp == 2, s8[16:], s16[16:])))
    win = jnp.where(grp == 0, 2, jnp.where(grp == 1, 4, jnp.where(grp == 2, 8, 16)))
    pos = i * tl + _row((tl, GROUP_W))
    cnt = jnp.minimum(win, pos + 1).astype(F32)
    d = pooled / cnt - u
    o_ref[...] = _mm(_act(d, nw == 2), _wts(w_refs)) * sc_ref[...]
    hist_ref[...] = ext[tl:]
    st_ref[0] = ext[tl:]


def pool_prompt(z_a, w_bd, scale, n_seq, seq_len, tl=512):
    nt = seq_len // tl
    return pl.pallas_call(
        functools.partial(_pool_kernel, tl=tl, nw=len(w_bd)),
        grid=(n_seq, nt),
        in_specs=[pl.BlockSpec((tl, GROUP_W), lambda s, i: (s * nt + i, 0)),
                  pl.BlockSpec((1, GROUP_W), lambda s, i: (0, 0))] + [_full(a, 2) for a in w_bd],
        out_specs=[pl.BlockSpec((tl, GROUP_W), lambda s, i: (s * nt + i, 0)),
                   pl.BlockSpec((1, 16, GROUP_W), lambda s, i: (s, 0, 0))],
        out_shape=[jax.ShapeDtypeStruct((n_seq * seq_len, GROUP_W), F32),
                   jax.ShapeDtypeStruct((n_seq, 16, GROUP_W), F32)],
        scratch_shapes=[pltpu.VMEM((16, GROUP_W), F32)],
        compiler_params=_cparams("parallel", "arbitrary"),
        name="pool_prompt",
    )(z_a, scale.reshape(1, GROUP_W), *w_bd)


def _conv_kernel(u_ref, cw_ref, cb_ref, lw_ref, lb_ref, pb_ref, *rest, tl, nw):
    w_refs, (o_ref, st_ref, ext_ref) = rest[:nw], rest[nw:]
    i = pl.program_id(1)

    @pl.when(i == 0)
    def _():
        ext_ref[0:32, :] = jnp.zeros((32, CONV_C), F32)

    u = u_ref[...]
    g = u[:, :CONV_C] * jax.nn.sigmoid(u[:, CONV_C:])
    ext_ref[32:, :] = g
    y = jnp.zeros((tl, CONV_C), F32)
    for j in range(CONV_W):
        y = y + ext_ref[2 + j:2 + j + tl, :] * cw_ref[j:j + 1, :]
    y = y + cb_ref[...]
    mu = jnp.mean(y, axis=-1, keepdims=True)
    yc = y - mu
    var = jnp.mean(yc * yc, axis=-1, keepdims=True)
    y = yc * lax.rsqrt(var + EPS) * lw_ref[...] + lb_ref[...]
    o_ref[...] = _mm(_act(jax.nn.silu(y), nw == 2), _wts(w_refs)) + pb_ref[...]
    tail = ext_ref[tl:tl + 32, :]
    st_ref[0] = tail
    ext_ref[0:32, :] = tail


def conv_prompt(z_d, cw, cb, lw, lb, pw, pb, n_seq, seq_len, tl=512):
    nt = seq_len // tl
    vec = lambda: pl.BlockSpec((1, CONV_C), lambda s, i: (0, 0))
    return pl.pallas_call(
        functools.partial(_conv_kernel, tl=tl, nw=len(pw)),
        grid=(n_seq, nt),
        in_specs=[pl.BlockSpec((tl, 2 * CONV_C), lambda s, i: (s * nt + i, 0)),
                  pl.BlockSpec((32, CONV_C), lambda s, i: (0, 0)),
                  vec(), vec(), vec(), vec()] + [_full(a, 2) for a in pw],
        out_specs=[pl.BlockSpec((tl, CONV_C), lambda s, i: (s * nt + i, 0)),
                   pl.BlockSpec((1, 32, CONV_C), lambda s, i: (s, 0, 0))],
        out_shape=[jax.ShapeDtypeStruct((n_seq * seq_len, CONV_C), F32),
                   jax.ShapeDtypeStruct((n_seq, 32, CONV_C), F32)],
        scratch_shapes=[pltpu.VMEM((tl + 32, CONV_C), F32)],
        compiler_params=_cparams("parallel", "arbitrary"),
        name="conv_prompt",
    )(z_d, cw, cb.reshape(1, -1), lw.reshape(1, -1), lb.reshape(1, -1), pb.reshape(1, -1), *pw)


def _split_pair(x):
    low = _lane(x.shape) < 64
    return jnp.where(low, x, 0.0), jnp.where(low, pltpu.roll(x, 64, 1), 0.0)


def _hgrn_chunk(zc, lb, onw, tri, s_ref, hp):
    c = zc.shape[0]
    nk = HG_HEADS * HG_DK
    q = zc[:, 0:nk]
    zf = zc[:, nk:2 * nk]
    v_all = zc[:, 2 * nk:2 * nk + 256]
    g_all = zc[:, 2 * nk + 256:2 * nk + 512]
    log_f = jnp.log(lb + (1.0 - lb) * jax.nn.sigmoid(zf))
    k = (1.0 - lb) * jax.nn.sigmoid(-zf)
    hi = log_f.astype(BF16)
    r1 = log_f - hi.astype(F32)
    mid = r1.astype(BF16)
    lo = (r1 - mid.astype(F32)).astype(BF16)
    b = _dot(tri, hi) + _dot(tri, mid) + _dot(tri, lo)
    b_last = b[c - 1:c, :]
    q_inter = q * jnp.exp(b)
    k_tail = k * jnp.exp(b_last - b)
    decay_last = jnp.exp(b_last)
    n_sub = c // HG_SUB
    rowc = _row((HG_SUB, LANES))
    outs = []
    for p in range(2):
        v_pair = _split_pair(v_all[:, LANES * p:LANES * (p + 1)])
        g_pair = _split_pair(g_all[:, LANES * p:LANES * (p + 1)])
        o_pair = []
        for hh in range(2):
            h = 2 * p + hh
            sl = slice(HG_DK * h, HG_DK * (h + 1))
            vh = v_pair[hh]
            vh_t = _act(vh, hp)
            s_old = s_ref[h]
            o = _mm(_act(q_inter[:, sl], hp), _act(s_old, hp))
            bh, qh, kh = b[:, sl], q[:, sl], k[:, sl]
            parts = []
            for ib in range(n_sub):
                r0 = ib * HG_SUB
                b_i, q_i = bh[r0:r0 + HG_SUB], qh[r0:r0 + HG_SUB]
                acc = o[r0:r0 + HG_SUB]
                if ib > 0:
                    ref_b = bh[r0 - 1:r0, :]
                    qt = q_i * jnp.exp(b_i - ref_b)
                    kt = kh[0:r0] * jnp.exp(ref_b - bh[0:r0])
                    a = _mm(_act(qt, hp), _act(kt, hp), nt=True)
                    acc = acc + _mm(_act(a, hp), tuple(t[0:r0] for t in vh_t))
                for s in range(HG_SUB):
                    r = r0 + s
                    e = jnp.exp(jnp.minimum(b_i - bh[r:r + 1, :], 0.0))
                    pr = jnp.where(rowc >= s, q_i * kh[r:r + 1, :] * e, 0.0)
                    acc = acc + jnp.sum(pr, axis=-1, keepdims=True) * vh[r:r + 1, :]
                parts.append(acc)
            o = jnp.concatenate(parts, axis=0) if n_sub > 1 else parts[0]
            dcol = jnp.broadcast_to(decay_last[:, sl], (LANES, LANES)).T
            s_ref[h] = dcol * s_old + _mm(_act(k_tail[:, sl].T, hp), vh_t)
            o = _rms(o, onw, HG_DV) * jax.nn.silu(g_pair[hh])
            o_pair.append(o)
        outs.append(o_pair[0] + pltpu.roll(o_pair[1], 64, 1))
    return jnp.concatenate(outs, axis=1)


def _hgrn_kernel(z_ref, lb_ref, onw_ref, tri_ref, o_ref, st_ref, s_ref, *, tl, hp):
    i = pl.program_id(1)

    @pl.when(i == 0)
    def _():
        s_ref[...] = jnp.zeros_like(s_ref)

    def body(cidx, carry):
        r0 = pl.multiple_of(cidx * HG_CHUNK, HG_CHUNK)
        zc = z_ref[pl.ds(r0, HG_CHUNK), :]
        o_ref[pl.ds(r0, HG_CHUNK), :] = _hgrn_chunk(zc, lb_ref[...], onw_ref[...], tri_ref[...], s_ref, hp)
        return carry

    lax.fori_loop(0, tl // HG_CHUNK, body, 0)
    st_ref[0] = s_ref[...]


def hgrn_prompt(z_c, lb, onw_slot, n_seq, seq_len, hp, tl=512):
    nt = seq_len // tl
    tri = jnp.tril(jnp.ones((HG_CHUNK, HG_CHUNK), F32)).astype(BF16)
    zw = z_c.shape[1]
    return pl.pallas_call(
        functools.partial(_hgrn_kernel, tl=tl, hp=hp),
        grid=(n_seq, nt),
        in_specs=[pl.BlockSpec((tl, zw), lambda s, i: (s * nt + i, 0)),
                  pl.BlockSpec((1, HG_HEADS * HG_DK), lambda s, i: (0, 0)),
                  pl.BlockSpec((1, LANES), lambda s, i: (0, 0)),
                  pl.BlockSpec((HG_CHUNK, HG_CHUNK), lambda s, i: (0, 0))],
        out_specs=[pl.BlockSpec((tl, GROUP_W), lambda s, i: (s * nt + i, 0)),
                   pl.BlockSpec((1, HG_HEADS, HG_DK, LANES), lambda s, i: (s, 0, 0, 0))],
        out_shape=[jax.ShapeDtypeStruct((n_seq * seq_len, GROUP_W), F32),
                   jax.ShapeDtypeStruct((n_seq, HG_HEADS, HG_DK, LANES), F32)],
        scratch_shapes=[pltpu.VMEM((HG_HEADS, HG_DK, LANES), F32)],
        compiler_params=_cparams("parallel", "arbitrary"),
        name="hgrn_prompt",
    )(z_c, lb.reshape(1, -1), onw_slot.reshape(1, LANES), tri)


def _mla_proj_kernel(z_ref, cs_ref, sn_ref, qnw_ref, qv_ref, qvs_ref, kvnw_ref, kv_ref, krv_ref, krvs_ref,
                     *rest, absorb, nw):
    wq_refs, wkv_refs, rest = rest[:nw], rest[nw:2 * nw], rest[2 * nw:]
    if absorb:
        wabs_ref, q_ref, k_ref, v_ref, ckv_ref, kr_ref, qabs_ref = rest
    else:
        q_ref, k_ref, v_ref, ckv_ref, kr_ref = rest
    hp = nw == 2
    z = z_ref[...]
    cos, sin = cs_ref[...], sn_ref[...]
    tm = z.shape[0]
    lane = _lane((tm, LANES))
    is_rope = lane < MLA_ROPE
    kr_raw = z[:, 384:512]
    kr_sw = z[:, 512:640]
    r_kr = lax.rsqrt(jnp.sum(kr_raw * kr_raw, axis=-1, keepdims=True) * (1.0 / MLA_ROPE) + EPS)
    k_rope = (kr_raw * r_kr * krv_ref[...]) * cos + (kr_sw * r_kr * krvs_ref[...]) * sin
    kr_ref[...] = k_rope
    c_kv = _rms(z[:, 256:384], kvnw_ref[...])
    ckv_ref[...] = c_kv
    kv = _mm(_act(c_kv, hp), _wts(wkv_refs))
    v_ref[...] = kv[:, 512:].astype(v_ref.dtype)
    ks = []
    for h in range(MLA_HEADS):
        kraw = kv[:, LANES * h:LANES * (h + 1)]
        ks.append(_rms(kraw, kv_ref[...], MLA_NOPE) + k_rope)
    k_ref[...] = jnp.concatenate(ks, axis=1).astype(k_ref.dtype)
    c_q = _rms(z[:, 0:256], qnw_ref[...])
    qq = _mm(_act(c_q, hp), _wts(wq_refs))
    qs = []
    for h in range(MLA_HEADS):
        x = qq[:, LANES * h:LANES * (h + 1)]
        xs = qq[:, 512 + LANES * h:512 + LANES * (h + 1)]
        x2 = x * x
        r_rope = lax.rsqrt(jnp.sum(jnp.where(is_rope, x2, 0.0), axis=-1, keepdims=True) * (1.0 / MLA_ROPE) + EPS)
        r_nope = lax.rsqrt(jnp.sum(jnp.where(is_rope, 0.0, x2), axis=-1, keepdims=True) * (1.0 / MLA_NOPE) + EPS)
        y = x * jnp.where(is_rope, r_rope, r_nope) * qv_ref[...]
        ysw = xs * r_rope * qvs_ref[...]
        qs.append((y * cos + ysw * sin) * MLA_SCALE)
    q = jnp.concatenate(qs, axis=1)
    q_ref[...] = q.astype(q_ref.dtype)
    if absorb:
        kvec4 = jnp.concatenate([kv_ref[...]] * MLA_HEADS, axis=1)
        qabs_ref[...] = _dot((q * kvec4).astype(BF16), wabs_ref[...])


def mla_proj(z_b, cos_t, sin_t, pw, row0, n_rows, absorb, hp, tile=TOK_TILE):
    assert row0 % tile == 0 and n_rows % tile == 0
    t0 = row0 // tile
    rows = lambda w: pl.BlockSpec((tile, w), lambda i: (i + t0, 0))
    out_rows = lambda w: pl.BlockSpec((tile, w), lambda i: (i, 0))
    consts = [pw["q_norm_w"], pw["qvec"], pw["qvec_sw"], pw["kv_norm_w"], pw["kvec"], pw["krvec"], pw["krvec_sw"]]
    consts += pw["wq"] + pw["wkv"]
    if absorb:
        consts.append(pw["wabs"])
    qkv_dt = F32 if hp else BF16
    out_shape = [jax.ShapeDtypeStruct((n_rows, 512), qkv_dt)] * 3 + [jax.ShapeDtypeStruct((n_rows, LANES), F32)] * 2
    out_specs = [out_rows(512)] * 3 + [out_rows(LANES)] * 2
    if absorb:
        out_shape.append(jax.ShapeDtypeStruct((n_rows, 512), F32))
        out_specs.append(out_rows(512))
    return pl.pallas_call(
        functools.partial(_mla_proj_kernel, absorb=absorb, nw=len(pw["wq"])),
        grid=(n_rows // tile,),
        in_specs=[rows(640), rows(LANES), rows(LANES)] + [_full(a, 1) for a in consts],
        out_specs=out_specs,
        out_shape=out_shape,
        compiler_params=_cparams("parallel"),
        name="mla_proj_s" if absorb else "mla_proj_p",
    )(z_b, cos_t, sin_t, *consts)


def _flash_kernel(qi_ref, kj_ref, q_ref, k_ref, v_ref, o_ref, m_ref, l_ref, acc_ref, *, tq, hp):
    n = pl.program_id(1)
    qi, kj = qi_ref[n], kj_ref[n]

    @pl.when(kj == 0)
    def _():
        m_ref[...] = jnp.full_like(m_ref, -jnp.inf)
        l_ref[...] = jnp.zeros_like(l_ref)
        acc_ref[...] = jnp.zeros_like(acc_ref)

    def operand(ref, sl):
        x = ref[:, sl]
        return _split(x) if hp else (x,)

    def step(masked):
        for h in range(MLA_HEADS):
            sl = slice(LANES * h, LANES * (h + 1))
            s = _mm(operand(q_ref, sl), operand(k_ref, sl), nt=True)
            if masked:
                s = jnp.where(_row((tq, tq)) >= _lane((tq, tq)), s, -jnp.inf)
            m_prev = m_ref[h]
            m_new = jnp.maximum(m_prev, jnp.max(s, axis=-1, keepdims=True))
            alpha = jnp.exp(m_prev - m_new)
            p = jnp.exp(s - jnp.tile(m_new, (1, tq // LANES)))
            l_ref[h] = alpha * l_ref[h] + jnp.sum(p, axis=-1, keepdims=True)
            acc_ref[h] = alpha * acc_ref[h] + _mm(_act(p, hp), operand(v_ref, sl))
            m_ref[h] = m_new

    @pl.when(kj < qi)
    def _():
        step(False)

    @pl.when(kj == qi)
    def _():
        step(True)
        o_ref[...] = jnp.concatenate([acc_ref[h] / l_ref[h] for h in range(MLA_HEADS)], axis=1)


def flash_prompt(q, k, v, n_seq, seq_len, hp, tq=512):
    nq = seq_len // tq
    qi = np.array([i for i in range(nq) for _ in range(i + 1)], np.int32)
    kj = np.array([j for i in range(nq) for j in range(i + 1)], np.int32)
    grid_spec = pltpu.PrefetchScalarGridSpec(
        num_scalar_prefetch=2,
        grid=(n_seq, len(qi)),
        in_specs=[pl.BlockSpec((tq, 512), lambda b, n, qi, kj: (b * nq + qi[n], 0)),
                  pl.BlockSpec((tq, 512), lambda b, n, qi, kj: (b * nq + kj[n], 0)),
                  pl.BlockSpec((tq, 512), lambda b, n, qi, kj: (b * nq + kj[n], 0))],
        out_specs=pl.BlockSpec((tq, 512), lambda b, n, qi, kj: (b * nq + qi[n], 0)),
        scratch_shapes=[pltpu.VMEM((MLA_HEADS, tq, LANES), F32), pltpu.VMEM((MLA_HEADS, tq, LANES), F32),
                        pltpu.VMEM((MLA_HEADS, tq, LANES), F32)])
    return pl.pallas_call(
        functools.partial(_flash_kernel, tq=tq, hp=hp),
        grid_spec=grid_spec,
        out_shape=jax.ShapeDtypeStruct((n_seq * seq_len, 512), F32),
        compiler_params=_cparams("parallel", "arbitrary"),
        name="mla_flash_prompt",
    )(jnp.asarray(qi), jnp.asarray(kj), q, k, v)


def _mla_sample_kernel(pt_ref, cache_c, cache_r, qa_ref, qr_ref, cn_ref, rn_ref, wuk_ref, seg_ref, o_ref,
                       m_ref, l_ref, acc_ref, cbuf, rbuf, csem, rsem, *, npg, n_steps, layer):
    g = pl.program_id(1)
    n = pl.program_id(0) * n_steps + g
    n_total = pl.num_programs(0) * n_steps
    slot = n % 2

    def page_copies(step, s, p):
        page = pt_ref[step * npg + p]
        return (pltpu.make_async_copy(cache_c.at[layer, page], cbuf.at[s, p], csem.at[s]),
                pltpu.make_async_copy(cache_r.at[layer, page], rbuf.at[s, p], rsem.at[s]))

    def start_pages(step, s):
        for p in range(npg):
            for cp in page_copies(step, s, p):
                cp.start()

    @pl.when(n == 0)
    def _():
        start_pages(0, 0)

    @pl.when(n + 1 < n_total)
    def _():
        start_pages(n + 1, 1 - slot)

    for p in range(npg):
        for cp in page_copies(n, slot, p):
            cp.wait()

    @pl.when(g == 0)
    def _():
        m_ref[...] = jnp.full_like(m_ref, -jnp.inf)
        l_ref[...] = jnp.zeros_like(l_ref)
        acc_ref[...] = jnp.zeros_like(acc_ref)

    qa = qa_ref[...].astype(BF16)
    qr = qr_ref[...][:, :MLA_ROPE].astype(BF16)
    wuk = wuk_ref[...]
    seg = seg_ref[...]

    def nope_scores(cb):
        kraw = _dot(cb, wuk)
        ssq = _dot_nt(seg, (kraw * kraw).astype(BF16))
        return _dot_nt(qa, cb) * lax.rsqrt(ssq * (1.0 / MLA_NOPE) + EPS)

    def update(s_all, cb):
        m_prev = m_ref[...]
        m_new = jnp.maximum(m_prev, jnp.max(s_all, axis=-1, keepdims=True))
        alpha = jnp.exp(m_prev - m_new)
        p_all = jnp.exp(s_all - m_new)
        l_ref[...] = alpha * l_ref[...] + jnp.sum(p_all, axis=-1, keepdims=True)
        acc_ref[...] = alpha * acc_ref[...] + _dot(p_all.astype(BF16), cb)
        m_ref[...] = m_new

    cb_all = jnp.concatenate([cbuf[slot, p].astype(BF16) for p in range(npg)], axis=0)
    kr_all = jnp.concatenate([rbuf[slot, p].astype(BF16) for p in range(npg)], axis=1)
    update(nope_scores(cb_all) + _dot(qr, kr_all), cb_all)

    @pl.when(g == n_steps - 1)
    def _():
        cb = cn_ref[...].astype(BF16)
        s_new = nope_scores(cb) + _dot_nt(qr, rn_ref[...][:, :MLA_ROPE].astype(BF16))
        qtok = _row(s_new.shape) // MLA_HEADS
        s_new = jnp.where(_lane(s_new.shape) <= qtok, s_new, -jnp.inf)
        update(s_new, cb)
        o_ref[...] = acc_ref[...] / l_ref[...]


def mla_sample(page_table, cache_c, cache_rt, layer, qa, qr, c_new, r_new, wuk, seg):
    nb, n_pages = page_table.shape
    npg = PAGES_PER_STEP
    n_steps = n_pages // npg
    rows = qa.shape[1]
    s_pad = c_new.shape[1]

    per_seq = lambda r, w: pl.BlockSpec((None, r, w), lambda b, g, pt: (b, 0, 0))
    full = lambda a: pl.BlockSpec(a.shape, lambda b, g, pt: (0,) * a.ndim)
    any_spec = pl.BlockSpec(memory_space=pl.ANY)
    grid_spec = pltpu.PrefetchScalarGridSpec(
        num_scalar_prefetch=1,
        grid=(nb, n_steps),
        in_specs=[any_spec, any_spec,
                  per_seq(rows, LANES), per_seq(rows, LANES), per_seq(s_pad, LANES), per_seq(s_pad, LANES),
                  full(wuk), full(seg)],
        out_specs=per_seq(rows, LANES),
        scratch_shapes=[pltpu.VMEM((rows, 1), F32), pltpu.VMEM((rows, 1), F32), pltpu.VMEM((rows, LANES), F32),
                        pltpu.VMEM((2, npg, PAGE_SIZE, MLA_KV_LORA), F32),
                        pltpu.VMEM((2, npg, MLA_ROPE, PAGE_SIZE), F32),
                        pltpu.SemaphoreType.DMA((2,)), pltpu.SemaphoreType.DMA((2,))])
    return pl.pallas_call(
        functools.partial(_mla_sample_kernel, npg=npg, n_steps=n_steps, layer=layer),
        grid_spec=grid_spec,
        out_shape=jax.ShapeDtypeStruct((nb, rows, LANES), F32),
        compiler_params=_cparams("arbitrary", "arbitrary"),
        name="mla_sample",
    )(page_table.reshape(-1), cache_c, cache_rt, qa, qr, c_new, r_new, wuk, seg)


def _matmul_kernel(a_ref, w_ref, o_ref):
    o_ref[...] = _dot(a_ref[...].astype(BF16), w_ref[...])


def matmul(a, w, tile=TOK_TILE, name="matmul"):
    t, kdim = a.shape
    n = w.shape[1]
    return pl.pallas_call(
        _matmul_kernel,
        grid=(t // tile,),
        in_specs=[pl.BlockSpec((tile, kdim), lambda i: (i, 0)), pl.BlockSpec((kdim, n), lambda i: (0, 0))],
        out_specs=pl.BlockSpec((tile, n), lambda i: (i, 0)),
        out_shape=jax.ShapeDtypeStruct((t, n), F32),
        compiler_params=_cparams("parallel"),
        name=name,
    )(a, w)


def _mix_out_kernel(oa_ref, ob_ref, oc_ref, od_ref, x_ref, ga_ref, gb_ref, gc_ref, gd_ref, n2_ref, *rest, nw):
    w_groups = [rest[nw * g:nw * (g + 1)] for g in range(5)]
    x1_ref, q_ref = rest[5 * nw:]
    hp = nw == 2
    acc = x_ref[...]
    for o_ref, g_ref, w_refs in ((oa_ref, ga_ref, w_groups[0]), (ob_ref, gb_ref, w_groups[1]),
                                 (oc_ref, gc_ref, w_groups[2]), (od_ref, gd_ref, w_groups[3])):
        acc = acc + _mm(_act(_rms(o_ref[...], g_ref[...], GROUP_W), hp), _wts(w_refs))
    x1_ref[...] = acc
    q_ref[...] = _mm(_act(_rms(acc, n2_ref[...]), hp), _wts(w_groups[4]))


def mix_out(o_a, o_b, o_c, o_d, x, gw, w_out, n2w, wq, tile=TOK_TILE):
    t = x.shape[0]
    rows = lambda a: pl.BlockSpec((tile, a.shape[1]), lambda i: (i, 0))
    acts = [o_a, o_b, o_c, o_d, x]
    consts = list(gw) + [n2w.reshape(1, -1)] + [a for w in w_out for a in w] + list(wq)
    return pl.pallas_call(
        functools.partial(_mix_out_kernel, nw=len(wq)),
        grid=(t // tile,),
        in_specs=[rows(a) for a in acts] + [_full(a, 1) for a in consts],
        out_specs=[pl.BlockSpec((tile, D_MODEL), lambda i: (i, 0)), pl.BlockSpec((tile, CA_W), lambda i: (i, 0))],
        out_shape=[jax.ShapeDtypeStruct((t, D_MODEL), F32), jax.ShapeDtypeStruct((t, CA_W), F32)],
        compiler_params=_cparams("parallel"),
        name="mix_out",
    )(*acts, *consts)


def _cross_kernel(q_ref, mk_ref, mv_ref, x_ref, qn_ref, *rest, nw):
    wo_refs, (o_ref,) = rest[:nw], rest[nw:]
    hp = nw == 2
    q = q_ref[...]
    outs = []
    for h in range(CA_HEADS):
        sl = slice(CA_HD * h, CA_HD * (h + 1))
        qh = _rms(q[:, sl], qn_ref[...])
        s = _mm(_act(qh, hp), _act(mk_ref[:, sl], hp), nt=True) * (CA_HD ** -0.5)
        s = s - jnp.max(s, axis=-1, keepdims=True)
        e = jnp.exp(s)
        p = e / jnp.sum(e, axis=-1, keepdims=True)
        outs.append(_mm(_act(p, hp), _act(mv_ref[:, sl], hp)))
    o = jnp.concatenate(outs, axis=1)
    o_ref[...] = x_ref[...] + _mm(_act(o, hp), _wts(wo_refs))


def cross_attend(q, mem_k, mem_v, x, qnw, wo, tl, layer=None):
    n_seq, seq_len, _ = q.shape
    nt = seq_len // tl
    if layer is None:
        mem_spec = pl.BlockSpec((None, N_MEM, CA_W), lambda s, i: (s, 0, 0))
    else:
        mem_spec = pl.BlockSpec((None, None, N_MEM, CA_W), lambda s, i: (layer, s, 0, 0))
    return pl.pallas_call(
        functools.partial(_cross_kernel, nw=len(wo)),
        grid=(n_seq, nt),
        in_specs=[pl.BlockSpec((None, tl, CA_W), lambda s, i: (s, i, 0)),
                  mem_spec, mem_spec,
                  pl.BlockSpec((None, tl, D_MODEL), lambda s, i: (s, i, 0)),
                  pl.BlockSpec((1, CA_HD), lambda s, i: (0, 0))] + [_full(a, 2) for a in wo],
        out_specs=pl.BlockSpec((None, tl, D_MODEL), lambda s, i: (s, i, 0)),
        out_shape=jax.ShapeDtypeStruct((n_seq, seq_len, D_MODEL), F32),
        compiler_params=_cparams("parallel", "parallel"),
        name="cross_attend",
    )(q, mem_k, mem_v, x, qnw.reshape(1, CA_HD), *wo)


def _swiglu_into(h, wg_refs, wu_refs, wd_refs, o_ref, width):
    hp = len(wg_refs) == 2
    for c0 in range(0, width, FF_CHUNK):
        cols = (slice(None), slice(c0, c0 + FF_CHUNK))
        g = _mm(h, _wts(wg_refs, cols))
        u = _mm(h, _wts(wu_refs, cols))
        a = _act(jax.nn.silu(g) * u, hp)
        o_ref[...] += _mm(a, _wts(wd_refs, (slice(c0, c0 + FF_CHUNK), slice(None))))


def _ffn_kernel(x_ref, nw_ref, *rest, nw, width):
    wg, wu, wd, (o_ref,) = rest[:nw], rest[nw:2 * nw], rest[2 * nw:3 * nw], rest[3 * nw:]
    x = x_ref[...]

    @pl.when(pl.program_id(1) == 0)
    def _():
        o_ref[...] = x

    _swiglu_into(_act(_rms(x, nw_ref[...]), nw == 2), wg, wu, wd, o_ref, width)


def ffn_dense(x, nw, wg, wu, wd, ff_block, tile=TOK_TILE):
    t = x.shape[0]
    up = lambda: pl.BlockSpec((D_MODEL, ff_block), lambda i, j: (0, j))
    down = lambda: pl.BlockSpec((ff_block, D_MODEL), lambda i, j: (j, 0))
    n = len(wg)
    return pl.pallas_call(
        functools.partial(_ffn_kernel, nw=n, width=ff_block),
        grid=(t // tile, D_FF // ff_block),
        in_specs=[pl.BlockSpec((tile, D_MODEL), lambda i, j: (i, 0)), pl.BlockSpec((1, D_MODEL), lambda i, j: (0, 0))]
                 + [up() for _ in range(2 * n)] + [down() for _ in range(n)],
        out_specs=pl.BlockSpec((tile, D_MODEL), lambda i, j: (i, 0)),
        out_shape=jax.ShapeDtypeStruct((t, D_MODEL), F32),
        compiler_params=_cparams("parallel", "arbitrary"),
        name="ffn_dense",
    )(x, nw.reshape(1, -1), *wg, *wu, *wd)


def _router_kernel(x_ref, nw_ref, r_ref, h_ref, g_ref):
    h = _rms(x_ref[...], nw_ref[...])
    h_ref[...] = h
    logits = jnp.dot(h, r_ref[...], preferred_element_type=F32, precision=lax.Precision.HIGHEST)
    lane = _lane(logits.shape)
    neg = -jnp.inf
    l1 = jnp.where(lane < N_EXPERTS, logits, neg)
    m1 = jnp.max(l1, axis=-1, keepdims=True)
    i1 = jnp.min(jnp.where(l1 == m1, lane, LANES), axis=-1, keepdims=True)
    l2 = jnp.where(lane == i1, neg, l1)
    m2 = jnp.max(l2, axis=-1, keepdims=True)
    i2 = jnp.min(jnp.where(l2 == m2, lane, LANES), axis=-1, keepdims=True)
    e2 = jnp.exp(m2 - m1)
    den = 1.0 + e2
    gates = jnp.where(lane == i1, 1.0 / den, jnp.where(lane == i2, e2 / den, 0.0))
    g_ref[...] = jnp.where(lane == N_EXPERTS, i1.astype(F32), jnp.where(lane == N_EXPERTS + 1, i2.astype(F32), gates))


def moe_router(x, nw, router_pad, tile=TOK_TILE):
    t = x.shape[0]
    return pl.pallas_call(
        _router_kernel,
        grid=(t // tile,),
        in_specs=[pl.BlockSpec((tile, D_MODEL), lambda i: (i, 0)), pl.BlockSpec((1, D_MODEL), lambda i: (0, 0)),
                  pl.BlockSpec((D_MODEL, LANES), lambda i: (0, 0))],
        out_specs=[pl.BlockSpec((tile, D_MODEL), lambda i: (i, 0)), pl.BlockSpec((tile, LANES), lambda i: (i, 0))],
        out_shape=[jax.ShapeDtypeStruct((t, D_MODEL), F32), jax.ShapeDtypeStruct((t, LANES), F32)],
        compiler_params=_cparams("parallel"),
        name="moe_router",
    )(x, nw.reshape(1, -1), router_pad)


def _experts_kernel(be_ref, nv_ref, src_ref, nxt_ref, h_hbm, wg_ref, wu_ref, wd_ref, o_ref, xbuf, sem, *, tile):
    b = pl.program_id(0)
    slot = b % 2
    nv = nv_ref[0]

    def row_copy(idx_ref, i, s):
        return pltpu.make_async_copy(h_hbm.at[pl.ds(idx_ref[0, i], 1), :], xbuf.at[s, pl.ds(i, 1), :], sem.at[s])

    def start_rows(idx_ref, s):
        def body(i, c):
            row_copy(idx_ref, i, s).start()
            return c
        lax.fori_loop(0, tile, body, 0, unroll=8)

    @pl.when((b == 0) & (nv > 0))
    def _():
        start_rows(src_ref, 0)

    @pl.when(b + 1 < nv)
    def _():
        start_rows(nxt_ref, 1 - slot)

    o_ref[...] = jnp.zeros_like(o_ref)

    @pl.when(b < nv)
    def _():
        def body(i, c):
            row_copy(src_ref, i, slot).wait()
            return c
        lax.fori_loop(0, tile, body, 0, unroll=8)
        _swiglu_into((xbuf[slot].astype(BF16),), (wg_ref,), (wu_ref,), (wd_ref,), o_ref, D_FF)


def moe_experts(h, src, blk_e, n_valid, wg, wu, wd, j, tile=MOE_TILE):
    n_blk = src.shape[0]
    wspec = lambda a: pl.BlockSpec((None, None) + a.shape[2:], lambda b, be, nv: (j, be[b], 0, 0))
    idx_spec = lambda f: pl.BlockSpec((None, 1, tile), f, memory_space=pltpu.SMEM)
    grid_spec = pltpu.PrefetchScalarGridSpec(
        num_scalar_prefetch=2,
        grid=(n_blk,),
        in_specs=[idx_spec(lambda b, be, nv: (b, 0, 0)),
                  idx_spec(lambda b, be, nv: (jnp.minimum(b + 1, n_blk - 1), 0, 0)),
                  pl.BlockSpec(memory_space=pl.ANY), wspec(wg), wspec(wu), wspec(wd)],
        out_specs=pl.BlockSpec((tile, D_MODEL), lambda b, be, nv: (b, 0)),
        scratch_shapes=[pltpu.VMEM((2, tile, D_MODEL), F32), pltpu.SemaphoreType.DMA((2,))])
    return pl.pallas_call(
        functools.partial(_experts_kernel, tile=tile),
        grid_spec=grid_spec,
        out_shape=jax.ShapeDtypeStruct((n_blk * tile, D_MODEL), F32),
        compiler_params=_cparams("arbitrary"),
        name="moe_experts",
    )(blk_e, n_valid, src, src, h, wg, wu, wd)


def moe_ffn(x, nw, router_pad, wg, wu, wd, j):
    t = x.shape[0]
    h, gfull = moe_router(x, nw, router_pad)
    e_idx = gfull[:, N_EXPERTS:N_EXPERTS + 2].astype(jnp.int32)
    g2 = jnp.take_along_axis(gfull[:, :N_EXPERTS], e_idx, axis=1)
    sel = (jnp.arange(N_EXPERTS, dtype=jnp.int32)[None, :] == e_idx[:, 0:1]) | (
        jnp.arange(N_EXPERTS, dtype=jnp.int32)[None, :] == e_idx[:, 1:2])
    pos = jnp.cumsum(sel.astype(jnp.int32), axis=0) - 1
    counts = pos[-1] + 1
    nb_e = (counts + MOE_TILE - 1) // MOE_TILE
    blk_end = jnp.cumsum(nb_e)
    row_start = (blk_end - nb_e) * MOE_TILE
    slot = row_start[None, :] + pos
    assert (2 * t) % MOE_TILE == 0
    n_blk = (2 * t + N_EXPERTS * MOE_TILE) // MOE_TILE
    stride = t + MOE_TILE
    big = N_EXPERTS * stride
    tok = jnp.arange(t, dtype=jnp.int32)[:, None]
    real = e_idx * stride + tok
    pad_e = nb_e * MOE_TILE - counts
    fill = jnp.arange(MOE_TILE, dtype=jnp.int32)[None, :]
    dummy = jnp.where(fill < pad_e[:, None],
                      jnp.arange(N_EXPERTS, dtype=jnp.int32)[:, None] * stride + t + fill, big)
    keys = jnp.sort(jnp.concatenate([real.reshape(-1), dummy.reshape(-1)]))
    src = keys % stride
    src = jnp.where((src >= t) | (keys >= big), 0, src).reshape(n_blk, 1, MOE_TILE)
    blk_e = jnp.minimum(jnp.searchsorted(blk_end, jnp.arange(n_blk, dtype=jnp.int32), side="right"),
                        N_EXPERTS - 1).astype(jnp.int32)
    y_buf = moe_experts(h, src, blk_e, blk_end[-1:].astype(jnp.int32), wg, wu, wd, j)
    s2 = jnp.take_along_axis(slot, e_idx, axis=1)
    y = y_buf[s2[:, 0]] * g2[:, 0:1] + y_buf[s2[:, 1]] * g2[:, 1:2]
    return x + y


def _zeros(r, c):
    return jnp.zeros((r, c), F32)


def _prep_w_in(w, hp):
    d = w.shape[0]
    k_r = w[:, 640:672]
    kr_slot = jnp.concatenate([k_r, _zeros(d, 96)], axis=1)
    kr_sw = jnp.concatenate([k_r[:, 16:], k_r[:, :16], _zeros(d, 96)], axis=1)
    return _w(jnp.concatenate([w[:, 0:256], w[:, 256:640], kr_slot, kr_sw, w[:, 672:2208], w[:, 2208:2720]],
                              axis=1), hp)


W_IN_SPLITS = (256, 640, 1536, 512)


def _slot_vec(rope_w, nope_w):
    z = jnp.zeros((32,), F32)
    return jnp.concatenate([rope_w, nope_w, z]).reshape(1, LANES)


def _prep_mla(lw, hp):
    wq, wkv = lw["mla_w_uq"], lw["mla_w_ukv"]
    slots, sw = [], []
    for h in range(MLA_HEADS):
        nope = wq[:, 96 * h:96 * h + 64]
        rope = wq[:, 96 * h + 64:96 * h + 96]
        slots.append(jnp.concatenate([rope, nope, _zeros(MLA_Q_LORA, 32)], axis=1))
        sw.append(jnp.concatenate([rope[:, 16:], rope[:, :16], _zeros(MLA_Q_LORA, 96)], axis=1))
    kslots = [jnp.concatenate([_zeros(128, 32), wkv[:, 128 * h:128 * h + 64], _zeros(128, 32)], axis=1)
              for h in range(MLA_HEADS)]
    vslots = [jnp.concatenate([wkv[:, 128 * h + 64:128 * h + 128], _zeros(128, 64)], axis=1)
              for h in range(MLA_HEADS)]
    z96 = jnp.zeros((96,), F32)
    qr, kr = lw["mla_qn_rope_w"], lw["mla_kn_rope_w"]
    wabs = jnp.zeros((512, 512), F32)
    for h in range(MLA_HEADS):
        wabs = wabs.at[128 * h + 32:128 * h + 96, 128 * h:128 * h + 128].set(wkv[:, 128 * h:128 * h + 64].T)
    wuv_bd = jnp.zeros((512, 256), F32)
    for h in range(MLA_HEADS):
        wuv_bd = wuv_bd.at[128 * h:128 * h + 128, 64 * h:64 * h + 64].set(wkv[:, 128 * h + 64:128 * h + 128])
    return dict(
        q_norm_w=lw["mla_q_norm_w"].reshape(1, -1),
        wq=_w(jnp.concatenate(slots + sw, axis=1), hp),
        qvec=_slot_vec(qr, lw["mla_qn_nope_w"]),
        qvec_sw=jnp.concatenate([qr[16:], qr[:16], z96]).reshape(1, LANES),
        kv_norm_w=lw["mla_kv_norm_w"].reshape(1, -1),
        wkv=_w(jnp.concatenate(kslots + vslots, axis=1), hp),
        kvec=_slot_vec(jnp.zeros((32,), F32), lw["mla_kn_nope_w"]),
        krvec=jnp.concatenate([kr, z96]).reshape(1, LANES),
        krvec_sw=jnp.concatenate([kr[16:], kr[:16], z96]).reshape(1, LANES),
        wabs=wabs.astype(BF16),
        wuk=jnp.concatenate([wkv[:, 128 * h:128 * h + 64] for h in range(MLA_HEADS)], axis=1).astype(BF16),
        wuv_bd=wuv_bd.astype(BF16),
    )


def _rope_tables(pos):
    half = MLA_ROPE // 2
    inv = 1.0 / (ROPE_THETA ** (jnp.arange(half, dtype=F32) / half))
    ang = pos.astype(F32)[:, None] * inv[None, :]
    cos, sin = jnp.cos(ang), jnp.sin(ang)
    n = pos.shape[0]
    cos_t = jnp.concatenate([cos, cos, jnp.ones((n, 64), F32), jnp.zeros((n, 32), F32)], axis=1)
    sin_t = jnp.concatenate([-sin, sin, jnp.zeros((n, 96), F32)], axis=1)
    return cos_t, sin_t


def _pad_new(a):
    n, s, w = a.shape
    return jnp.concatenate([a, jnp.zeros((n, 8 - s, w), a.dtype)], axis=1)


def _pad_rows(w, rows_per, pad_to):
    g = w.shape[0] // rows_per
    w = w.reshape(g, rows_per, -1)
    return jnp.concatenate([w, jnp.zeros((g, pad_to - rows_per, w.shape[-1]), w.dtype)], axis=1).reshape(
        g * pad_to, -1)


def _sample_pool(u, prefix, w_bd, scale):
    n, L, c = u.shape
    ext = jnp.concatenate([prefix, u], axis=1)
    cs = jnp.concatenate([jnp.zeros((n, 1, c), F32), jnp.cumsum(ext, axis=1)], axis=1)
    hi = cs[:, POOL_PAD + 1:]
    pooled = []
    for g, w in enumerate(POOL_WINDOWS):
        sl = slice(64 * g, 64 * (g + 1))
        lo = cs[:, POOL_PAD + 1 - w:POOL_PAD + 1 - w + L, sl]
        pooled.append((hi[..., sl] - lo) / float(w))
    d = jnp.concatenate(pooled, axis=-1) - u
    y = jnp.dot(d.reshape(n * L, c).astype(BF16), w_bd, preferred_element_type=F32) * scale
    return y, ext[:, -POOL_PAD:]


def _sample_hgrn(zc, s0, lb, onw):
    n, L, _ = zc.shape
    q = zc[..., 0:512].reshape(n, L, HG_HEADS, HG_DK)
    zf = zc[..., 512:1024].reshape(n, L, HG_HEADS, HG_DK)
    v = zc[..., 1024:1280].reshape(n, L, HG_HEADS, HG_DV)
    g = zc[..., 1280:1536].reshape(n, L, HG_HEADS, HG_DV)
    lbh = lb.reshape(HG_HEADS, HG_DK)
    f = lbh + (1.0 - lbh) * jax.nn.sigmoid(zf)
    k = (1.0 - lbh) * jax.nn.sigmoid(-zf)
    s = s0
    outs = []
    for t in range(L):
        s = f[:, t][..., None] * s + k[:, t][..., None] * v[:, t][:, :, None, :]
        outs.append(jnp.sum(s * q[:, t][..., None], axis=2))
    o = jnp.stack(outs, axis=1)
    o = _rms(o, onw, HG_DV) * jax.nn.silu(g)
    return o.reshape(n * L, GROUP_W), s


def _sample_conv(u, prefix, cw, cb, lw, lb, pw, pb):
    n, L, _ = u.shape
    g = u[..., :CONV_C] * jax.nn.sigmoid(u[..., CONV_C:])
    ext = jnp.concatenate([prefix, g], axis=1)
    y = sum(ext[:, j:j + L, :] * cw[j][None, None, :] for j in range(CONV_W)) + cb
    mu = jnp.mean(y, axis=-1, keepdims=True)
    yc = y - mu
    var = jnp.mean(yc * yc, axis=-1, keepdims=True)
    y = yc * lax.rsqrt(var + EPS) * lw + lb
    y = jnp.dot(jax.nn.silu(y).reshape(n * L, CONV_C).astype(BF16), pw, preferred_element_type=F32) + pb
    return y, ext[:, -CONV_PAD:]


def kernel(x_prompt, x_sample, cache_kv_latent, cache_k_rope, cache_mem_k, cache_mem_v, state_pool, state_hgrn, state_conv, page_table, mem_prompt, norm1_w, w_in, pool_w, pool_scale, mla_q_norm_w, mla_w_uq, mla_kv_norm_w, mla_w_ukv, mla_qn_nope_w, mla_qn_rope_w, mla_kn_nope_w, mla_kn_rope_w, hg_lb_param, hg_onorm_w, conv_w, conv_b, conv_ln_w, conv_ln_b, conv_pw_w, conv_pw_b, grp_norm_w, w_out, norm2_w, mem_norm_w, ca_wq, ca_wk, ca_wv, ca_qn_w, ca_kn_w, ca_wo, norm3_w, ffn_w_gate, ffn_w_up, ffn_w_down, moe_router, moe_w_gate, moe_w_up, moe_w_down):
    bp, seq, d = x_prompt.shape
    nb, s_len, _ = x_sample.shape
    depth = w_in.shape[0]
    tp, ts = bp * seq, nb * s_len
    n_past = page_table.shape[1] * PAGE_SIZE
    n_mem = mem_prompt.shape[1]
    first_moe = 1

    sm = jax.nn.softmax(hg_lb_param.astype(F32), axis=0)
    lower_bounds = jnp.cumsum(sm, axis=0) - sm[:1]

    pos = jnp.concatenate([jnp.tile(jnp.arange(seq, dtype=jnp.int32), bp),
                           jnp.tile(n_past + jnp.arange(s_len, dtype=jnp.int32), nb)])
    cos_t, sin_t = _rope_tables(pos)
    seg = (jnp.arange(256)[None, :] // 64 == jnp.arange(MLA_HEADS * s_len)[:, None] % MLA_HEADS).astype(BF16)
    cache_rt = jnp.swapaxes(cache_k_rope, 2, 3)
    moe_wg, moe_wu, moe_wd = moe_w_gate.astype(BF16), moe_w_up.astype(BF16), moe_w_down.astype(BF16)
    mem_k_all = cache_mem_k.reshape(depth, nb, n_mem, CA_W)
    mem_v_all = cache_mem_v.reshape(depth, nb, n_mem, CA_W)

    x = jnp.concatenate([x_prompt.reshape(tp, d), x_sample.reshape(ts, d)], axis=0)
    mem2d = mem_prompt.reshape(bp * n_mem, d)
    outs = [[] for _ in range(12)]

    for l in range(depth):
        hp = l <= first_moe
        lw = dict(mla_q_norm_w=mla_q_norm_w[l], mla_w_uq=mla_w_uq[l], mla_kv_norm_w=mla_kv_norm_w[l],
                  mla_w_ukv=mla_w_ukv[l], mla_qn_nope_w=mla_qn_nope_w[l], mla_qn_rope_w=mla_qn_rope_w[l],
                  mla_kn_nope_w=mla_kn_nope_w[l], mla_kn_rope_w=mla_kn_rope_w[l])
        pw = _prep_mla(lw, hp)
        z_a, z_b, z_c, z_d = norm_matmul(x, norm1_w[l], _prep_w_in(w_in[l], hp), W_IN_SPLITS, name="norm_w_in")

        w_bd = _w(jax.scipy.linalg.block_diag(*[pool_w[l, g] for g in range(4)]), hp)
        o_a_p, pool_st = pool_prompt(z_a, w_bd, pool_scale[l], bp, seq)
        o_a_s, pool_st_s = _sample_pool(z_a[tp:].reshape(nb, s_len, GROUP_W), state_pool[l], w_bd[0], pool_scale[l])
        outs[0].append(pool_st[:, 1:])
        outs[7].append(pool_st_s)

        q_p, k_p, v_p, ckv_p, kr_p = mla_proj(z_b, cos_t, sin_t, pw, 0, tp, absorb=False, hp=hp)
        o_b_p = flash_prompt(q_p, k_p, v_p, bp, seq, hp)
        q_s, _, _, ckv_s, kr_s, qabs_s = mla_proj(z_b, cos_t, sin_t, pw, tp, ts, absorb=True, hp=hp)
        rows = MLA_HEADS * s_len
        o_lat = mla_sample(page_table, cache_kv_latent, cache_rt, l,
                           qabs_s.reshape(nb, rows, LANES), q_s.astype(F32).reshape(nb, rows, LANES),
                           _pad_new(ckv_s.reshape(nb, s_len, LANES)), _pad_new(kr_s.reshape(nb, s_len, LANES)),
                           pw["wuk"], seg)
        o_b_s = matmul(o_lat.reshape(ts, 512), pw["wuv_bd"], name="mla_v_up")
        o_b_s = jnp.concatenate([o_b_s.reshape(ts, 4, 64), jnp.zeros((ts, 4, 64), F32)], axis=-1).reshape(ts, 512)
        outs[1].append(ckv_p.reshape(bp, seq, MLA_KV_LORA))
        outs[2].append(kr_p[:, :MLA_ROPE].reshape(bp, seq, MLA_ROPE))
        outs[8].append(ckv_s.reshape(nb, s_len, MLA_KV_LORA))
        outs[9].append(kr_s[:, :MLA_ROPE].reshape(nb, s_len, MLA_ROPE))

        onw_slot = jnp.concatenate([hg_onorm_w[l], jnp.zeros((64,), F32)])
        o_c_p, hg_st = hgrn_prompt(z_c, lower_bounds[l], onw_slot, bp, seq, hp)
        o_c_s, hg_st_s = _sample_hgrn(z_c[tp:].reshape(nb, s_len, -1), state_hgrn[l], lower_bounds[l], hg_onorm_w[l])
        outs[3].append(hg_st[..., :HG_DV])
        outs[10].append(hg_st_s)

        cw_pad = jnp.concatenate([conv_w[l], jnp.zeros((1, CONV_C), F32)], axis=0)
        pw_c = _w(conv_pw_w[l], hp)
        o_d_p, conv_st = conv_prompt(z_d, cw_pad, conv_b[l], conv_ln_w[l], conv_ln_b[l], pw_c, conv_pw_b[l], bp, seq)
        o_d_s, conv_st_s = _sample_conv(z_d[tp:].reshape(nb, s_len, -1), state_conv[l], conv_w[l], conv_b[l],
                                        conv_ln_w[l], conv_ln_b[l], pw_c[0], conv_pw_b[l])
        outs[4].append(conv_st[:, 2:])
        outs[11].append(conv_st_s)

        o_a = jnp.concatenate([o_a_p, o_a_s], axis=0)
        o_b = jnp.concatenate([o_b_p, o_b_s], axis=0)
        o_c = jnp.concatenate([o_c_p, o_c_s], axis=0)
        o_d = jnp.concatenate([o_d_p, o_d_s], axis=0)
        gnw = grp_norm_w[l]
        gw = [gnw[0:256].reshape(1, -1), _pad_rows(gnw[256:512].reshape(-1, 1), 64, 128).reshape(1, -1),
              gnw[512:768].reshape(1, -1), gnw[768:1024].reshape(1, -1)]
        wo_l = w_out[l]
        w_out_parts = [_w(wo_l[0:256], hp), _w(_pad_rows(wo_l[256:512], 64, 128), hp),
                       _w(wo_l[512:768], hp), _w(wo_l[768:1024], hp)]
        x1, q_ca = mix_out(o_a, o_b, o_c, o_d, x, gw, w_out_parts, norm2_w[l], _w(ca_wq[l], hp))

        w_kv = _w(jnp.concatenate([ca_wk[l], ca_wv[l]], axis=1), hp)
        mk_p, mv_p = norm_matmul(mem2d, mem_norm_w[l], w_kv, (CA_W, CA_W), slot_norm_w=ca_kn_w[l],
                                 tile=min(TOK_TILE, mem2d.shape[0]), name="memory_kv")
        outs[5].append(mk_p.reshape(bp, n_mem, CA_HEADS, CA_HD))
        outs[6].append(mv_p.reshape(bp, n_mem, CA_HEADS, CA_HD))

        wo_ca = _w(ca_wo[l], hp)
        x2_p = cross_attend(q_ca[:tp].reshape(bp, seq, CA_W), mk_p.reshape(bp, n_mem, CA_W),
                            mv_p.reshape(bp, n_mem, CA_W), x1[:tp].reshape(bp, seq, d), ca_qn_w[l], wo_ca,
                            tl=min(512, seq))
        x2_s = cross_attend(q_ca[tp:].reshape(nb, s_len, CA_W), mem_k_all, mem_v_all,
                            x1[tp:].reshape(nb, s_len, d), ca_qn_w[l], wo_ca, tl=s_len, layer=l)
        x2 = jnp.concatenate([x2_p.reshape(tp, d), x2_s.reshape(ts, d)], axis=0)

        j = l // 2
        if l % 2 == 0:
            x = ffn_dense(x2, norm3_w[l], _w(ffn_w_gate[j], hp), _w(ffn_w_up[j], hp), _w(ffn_w_down[j], hp),
                          ff_block=FF_CHUNK if hp else D_FF)
        else:
            router_pad = jnp.concatenate([moe_router[j], jnp.zeros((d, LANES - N_EXPERTS), F32)], axis=1)
            x = moe_ffn(x2, norm3_w[l], router_pad, moe_wg, moe_wu, moe_wd, j)

    st = lambda k: jnp.stack(outs[k])
    return (x[:tp].reshape(bp, seq, d), x[tp:].reshape(nb, s_len, d),
            st(0), st(1), st(2), st(3), st(4), st(5), st(6), st(7), st(8), st(9), st(10), st(11))
```

```python
import functools

import numpy as np
import jax
import jax.numpy as jnp
from jax import lax
from jax.experimental import pallas as pl
from jax.experimental.pallas import tpu as pltpu

F32 = jnp.float32
BF16 = jnp.bfloat16
EPS = 1e-6

D_MODEL = 1024
GROUP_W = 256
POOL_WINDOWS = (2, 4, 8, 16)
POOL_PAD = 15
MLA_HEADS = 4
MLA_NOPE = 64
MLA_ROPE = 32
MLA_V = 64
MLA_Q_LORA = 256
MLA_KV_LORA = 128
MLA_SCALE = (MLA_NOPE + MLA_ROPE) ** -0.5
ROPE_THETA = 10000.0
PAGE_SIZE = 128
HG_HEADS = 4
HG_DK = 128
HG_DV = 64
CONV_W = 31
CONV_PAD = 30
CONV_C = 256
N_MEM = 256
CA_HEADS = 4
CA_HD = 128
CA_W = 512
D_FF = 2816
N_EXPERTS = 8

LANES = 128
VMEM_LIMIT_BYTES = 56 * 1024 * 1024
TOK_TILE = 512
FF_CHUNK = 256
MOE_TILE = 512
HG_CHUNK = 64
HG_SUB = 16
PAGES_PER_STEP = 16


def _cparams(*sem):
    return pltpu.CompilerParams(dimension_semantics=sem, vmem_limit_bytes=VMEM_LIMIT_BYTES)


def _rms(x, w, n=None):
    n = x.shape[-1] if n is None else n
    r = lax.rsqrt(jnp.sum(x * x, axis=-1, keepdims=True) * (1.0 / n) + EPS)
    return x * r * w


def _dot(a, b):
    return jnp.dot(a, b, preferred_element_type=F32)


def _dot_nt(a, b):
    return lax.dot_general(a, b, (((1,), (1,)), ((), ())), preferred_element_type=F32)


def _lane(shape):
    return lax.broadcasted_iota(jnp.int32, shape, len(shape) - 1)


def _row(shape):
    return lax.broadcasted_iota(jnp.int32, shape, len(shape) - 2)


def _split(a):
    hi = a.astype(BF16)
    return hi, (a - hi.astype(F32)).astype(BF16)


def _act(a, hp):
    return _split(a) if hp else (a.astype(BF16),)


def _wts(refs, idx=None):
    return tuple(r[...] if idx is None else r[idx] for r in refs)


def _mm(a, b, nt=False):
    d = _dot_nt if nt else _dot
    out = d(a[0], b[0])
    if len(a) > 1:
        out = out + d(a[1], b[0]) + d(a[0], b[1])
    return out


def _w(w, hp):
    return list(_split(w)) if hp else [w.astype(BF16)]


def _full(a, nargs):
    zeros = (0,) * a.ndim
    if nargs == 1:
        return pl.BlockSpec(a.shape, lambda i: zeros)
    if nargs == 2:
        return pl.BlockSpec(a.shape, lambda i, j: zeros)
    return pl.BlockSpec(a.shape, lambda i, j, k: zeros)


def _norm_matmul_kernel(x_ref, nw_ref, *rest, splits, slot_norm, nw):
    w_refs, rest = rest[:nw], rest[nw:]
    if slot_norm:
        sw_ref, out_refs = rest[0], rest[1:]
    else:
        out_refs = rest
    h = _act(_rms(x_ref[...], nw_ref[...]), nw == 2)
    off = 0
    for idx, (o_ref, n) in enumerate(zip(out_refs, splits)):
        y = _mm(h, _wts(w_refs, (slice(None), slice(off, off + n))))
        if slot_norm and idx == 0:
            y = jnp.concatenate(
                [_rms(y[:, s:s + LANES], sw_ref[...]) for s in range(0, n, LANES)], axis=1)
        o_ref[...] = y
        off += n


def norm_matmul(x, nw, w, splits, slot_norm_w=None, tile=TOK_TILE, name="norm_matmul"):
    t, d = x.shape
    n = w[0].shape[1]
    assert sum(splits) == n and t % tile == 0
    in_specs = [pl.BlockSpec((tile, d), lambda i: (i, 0)), pl.BlockSpec((1, d), lambda i: (0, 0))]
    in_specs += [_full(a, 1) for a in w]
    args = [x, nw.reshape(1, d)] + list(w)
    if slot_norm_w is not None:
        in_specs.append(pl.BlockSpec((1, LANES), lambda i: (0, 0)))
        args.append(slot_norm_w.reshape(1, LANES))
    return pl.pallas_call(
        functools.partial(_norm_matmul_kernel, splits=splits, slot_norm=slot_norm_w is not None, nw=len(w)),
        grid=(t // tile,),
        in_specs=in_specs,
        out_specs=[pl.BlockSpec((tile, s), lambda i: (i, 0)) for s in splits],
        out_shape=[jax.ShapeDtypeStruct((t, s), F32) for s in splits],
        compiler_params=_cparams("parallel"),
        name=name,
    )(*args)


def _pool_kernel(u_ref, sc_ref, *rest, tl, nw):
    w_refs, (o_ref, st_ref, hist_ref) = rest[:nw], rest[nw:]
    i = pl.program_id(1)

    @pl.when(i == 0)
    def _():
        hist_ref[...] = jnp.zeros_like(hist_ref)

    u = u_ref[...]
    ext = jnp.concatenate([hist_ref[...], u], axis=0)
    s2 = ext + pltpu.roll(ext, 1, 0)
    s4 = s2 + pltpu.roll(s2, 2, 0)
    s8 = s4 + pltpu.roll(s4, 4, 0)
    s16 = s8 + pltpu.roll(s8, 8, 0)
    lane = _lane((tl, GROUP_W))
    grp = lane // 64
    pooled = jnp.where(grp == 0, s2[16:], jnp.where(grp == 1, s4[16:], jnp.where(grp == 2, s8[16:], s16[16:])))
    win = jnp.where(grp == 0, 2, jnp.where(grp == 1, 4, jnp.where(grp == 2, 8, 16)))
    pos = i * tl + _row((tl, GROUP_W))
    cnt = jnp.minimum(win, pos + 1).astype(F32)
    d = pooled / cnt - u
    o_ref[...] = _mm(_act(d, nw == 2), _wts(w_refs)) * sc_ref[...]
    hist_ref[...] = ext[tl:]
    st_ref[0] = ext[tl:]


def pool_prompt(z_a, w_bd, scale, n_seq, seq_len, tl=512):
    nt = seq_len // tl
    return pl.pallas_call(
        functools.partial(_pool_kernel, tl=tl, nw=len(w_bd)),
        grid=(n_seq, nt),
        in_specs=[pl.BlockSpec((tl, GROUP_W), lambda s, i: (s * nt + i, 0)),
                  pl.BlockSpec((1, GROUP_W), lambda s, i: (0, 0))] + [_full(a, 2) for a in w_bd],
        out_specs=[pl.BlockSpec((tl, GROUP_W), lambda s, i: (s * nt + i, 0)),
                   pl.BlockSpec((1, 16, GROUP_W), lambda s, i: (s, 0, 0))],
        out_shape=[jax.ShapeDtypeStruct((n_seq * seq_len, GROUP_W), F32),
                   jax.ShapeDtypeStruct((n_seq, 16, GROUP_W), F32)],
        scratch_shapes=[pltpu.VMEM((16, GROUP_W), F32)],
        compiler_params=_cparams("parallel", "arbitrary"),
        name="pool_prompt",
    )(z_a, scale.reshape(1, GROUP_W), *w_bd)


def _conv_kernel(u_ref, cw_ref, cb_ref, lw_ref, lb_ref, pb_ref, *rest, tl, nw):
    w_refs, (o_ref, st_ref, ext_ref) = rest[:nw], rest[nw:]
    i = pl.program_id(1)

    @pl.when(i == 0)
    def _():
        ext_ref[0:32, :] = jnp.zeros((32, CONV_C), F32)

    u = u_ref[...]
    g = u[:, :CONV_C] * jax.nn.sigmoid(u[:, CONV_C:])
    ext_ref[32:, :] = g
    y = jnp.zeros((tl, CONV_C), F32)
    for j in range(CONV_W):
        y = y + ext_ref[2 + j:2 + j + tl, :] * cw_ref[j:j + 1, :]
    y = y + cb_ref[...]
    mu = jnp.mean(y, axis=-1, keepdims=True)
    yc = y - mu
    var = jnp.mean(yc * yc, axis=-1, keepdims=True)
    y = yc * lax.rsqrt(var + EPS) * lw_ref[...] + lb_ref[...]
    o_ref[...] = _mm(_act(jax.nn.silu(y), nw == 2), _wts(w_refs)) + pb_ref[...]
    tail = ext_ref[tl:tl + 32, :]
    st_ref[0] = tail
    ext_ref[0:32, :] = tail


def conv_prompt(z_d, cw, cb, lw, lb, pw, pb, n_seq, seq_len, tl=512):
    nt = seq_len // tl
    vec = lambda: pl.BlockSpec((1, CONV_C), lambda s, i: (0, 0))
    return pl.pallas_call(
        functools.partial(_conv_kernel, tl=tl, nw=len(pw)),
        grid=(n_seq, nt),
        in_specs=[pl.BlockSpec((tl, 2 * CONV_C), lambda s, i: (s * nt + i, 0)),
                  pl.BlockSpec((32, CONV_C), lambda s, i: (0, 0)),
                  vec(), vec(), vec(), vec()] + [_full(a, 2) for a in pw],
        out_specs=[pl.BlockSpec((tl, CONV_C), lambda s, i: (s * nt + i, 0)),
                   pl.BlockSpec((1, 32, CONV_C), lambda s, i: (s, 0, 0))],
        out_shape=[jax.ShapeDtypeStruct((n_seq * seq_len, CONV_C), F32),
                   jax.ShapeDtypeStruct((n_seq, 32, CONV_C), F32)],
        scratch_shapes=[pltpu.VMEM((tl + 32, CONV_C), F32)],
        compiler_params=_cparams("parallel", "arbitrary"),
        name="conv_prompt",
    )(z_d, cw, cb.reshape(1, -1), lw.reshape(1, -1), lb.reshape(1, -1), pb.reshape(1, -1), *pw)


def _split_pair(x):
    low = _lane(x.shape) < 64
    return jnp.where(low, x, 0.0), jnp.where(low, pltpu.roll(x, 64, 1), 0.0)


def _hgrn_chunk(zc, lb, onw, tri, s_ref, hp):
    c = zc.shape[0]
    nk = HG_HEADS * HG_DK
    q = zc[:, 0:nk]
    zf = zc[:, nk:2 * nk]
    v_all = zc[:, 2 * nk:2 * nk + 256]
    g_all = zc[:, 2 * nk + 256:2 * nk + 512]
    log_f = jnp.log(lb + (1.0 - lb) * jax.nn.sigmoid(zf))
    k = (1.0 - lb) * jax.nn.sigmoid(-zf)
    hi = log_f.astype(BF16)
    r1 = log_f - hi.astype(F32)
    mid = r1.astype(BF16)
    lo = (r1 - mid.astype(F32)).astype(BF16)
    b = _dot(tri, hi) + _dot(tri, mid) + _dot(tri, lo)
    b_last = b[c - 1:c, :]
    q_inter = q * jnp.exp(b)
    k_tail = k * jnp.exp(b_last - b)
    decay_last = jnp.exp(b_last)
    n_sub = c // HG_SUB
    rowc = _row((HG_SUB, LANES))
    outs = []
    for p in range(2):
        v_pair = _split_pair(v_all[:, LANES * p:LANES * (p + 1)])
        g_pair = _split_pair(g_all[:, LANES * p:LANES * (p + 1)])
        o_pair = []
        for hh in range(2):
            h = 2 * p + hh
            sl = slice(HG_DK * h, HG_DK * (h + 1))
            vh = v_pair[hh]
            vh_t = _act(vh, hp)
            s_old = s_ref[h]
            o = _mm(_act(q_inter[:, sl], hp), _act(s_old, hp))
            bh, qh, kh = b[:, sl], q[:, sl], k[:, sl]
            parts = []
            for ib in range(n_sub):
                r0 = ib * HG_SUB
                b_i, q_i = bh[r0:r0 + HG_SUB], qh[r0:r0 + HG_SUB]
                acc = o[r0:r0 + HG_SUB]
                if ib > 0:
                    ref_b = bh[r0 - 1:r0, :]
                    qt = q_i * jnp.exp(b_i - ref_b)
                    kt = kh[0:r0] * jnp.exp(ref_b - bh[0:r0])
                    a = _mm(_act(qt, hp), _act(kt, hp), nt=True)
                    acc = acc + _mm(_act(a, hp), tuple(t[0:r0] for t in vh_t))
                for s in range(HG_SUB):
                    r = r0 + s
                    e = jnp.exp(jnp.minimum(b_i - bh[r:r + 1, :], 0.0))
                    pr = jnp.where(rowc >= s, q_i * kh[r:r + 1, :] * e, 0.0)
                    acc = acc + jnp.sum(pr, axis=-1, keepdims=True) * vh[r:r + 1, :]
                parts.append(acc)
            o = jnp.concatenate(parts, axis=0) if n_sub > 1 else parts[0]
            dcol = jnp.broadcast_to(decay_last[:, sl], (LANES, LANES)).T
            s_ref[h] = dcol * s_old + _mm(_act(k_tail[:, sl].T, hp), vh_t)
            o = _rms(o, onw, HG_DV) * jax.nn.silu(g_pair[hh])
            o_pair.append(o)
        outs.append(o_pair[0] + pltpu.roll(o_pair[1], 64, 1))
    return jnp.concatenate(outs, axis=1)


def _hgrn_kernel(z_ref, lb_ref, onw_ref, tri_ref, o_ref, st_ref, s_ref, *, tl, hp):
    i = pl.program_id(1)

    @pl.when(i == 0)
    def _():
        s_ref[...] = jnp.zeros_like(s_ref)

    def body(cidx, carry):
        r0 = pl.multiple_of(cidx * HG_CHUNK, HG_CHUNK)
        zc = z_ref[pl.ds(r0, HG_CHUNK), :]
        o_ref[pl.ds(r0, HG_CHUNK), :] = _hgrn_chunk(zc, lb_ref[...], onw_ref[...], tri_ref[...], s_ref, hp)
        return carry

    lax.fori_loop(0, tl // HG_CHUNK, body, 0)
    st_ref[0] = s_ref[...]


def hgrn_prompt(z_c, lb, onw_slot, n_seq, seq_len, hp, tl=512):
    nt = seq_len // tl
    tri = jnp.tril(jnp.ones((HG_CHUNK, HG_CHUNK), F32)).astype(BF16)
    zw = z_c.shape[1]
    return pl.pallas_call(
        functools.partial(_hgrn_kernel, tl=tl, hp=hp),
        grid=(n_seq, nt),
        in_specs=[pl.BlockSpec((tl, zw), lambda s, i: (s * nt + i, 0)),
                  pl.BlockSpec((1, HG_HEADS * HG_DK), lambda s, i: (0, 0)),
                  pl.BlockSpec((1, LANES), lambda s, i: (0, 0)),
                  pl.BlockSpec((HG_CHUNK, HG_CHUNK), lambda s, i: (0, 0))],
        out_specs=[pl.BlockSpec((tl, GROUP_W), lambda s, i: (s * nt + i, 0)),
                   pl.BlockSpec((1, HG_HEADS, HG_DK, LANES), lambda s, i: (s, 0, 0, 0))],
        out_shape=[jax.ShapeDtypeStruct((n_seq * seq_len, GROUP_W), F32),
                   jax.ShapeDtypeStruct((n_seq, HG_HEADS, HG_DK, LANES), F32)],
        scratch_shapes=[pltpu.VMEM((HG_HEADS, HG_DK, LANES), F32)],
        compiler_params=_cparams("parallel", "arbitrary"),
        name="hgrn_prompt",
    )(z_c, lb.reshape(1, -1), onw_slot.reshape(1, LANES), tri)


def _mla_proj_kernel(z_ref, cs_ref, sn_ref, qnw_ref, qv_ref, qvs_ref, kvnw_ref, kv_ref, krv_ref, krvs_ref,
                     *rest, absorb, nw):
    wq_refs, wkv_refs, rest = rest[:nw], rest[nw:2 * nw], rest[2 * nw:]
    if absorb:
        wabs_ref, q_ref, k_ref, v_ref, ckv_ref, kr_ref, qabs_ref = rest
    else:
        q_ref, k_ref, v_ref, ckv_ref, kr_ref = rest
    hp = nw == 2
    z = z_ref[...]
    cos, sin = cs_ref[...], sn_ref[...]
    tm = z.shape[0]
    lane = _lane((tm, LANES))
    is_rope = lane < MLA_ROPE
    kr_raw = z[:, 384:512]
    kr_sw = z[:, 512:640]
    r_kr = lax.rsqrt(jnp.sum(kr_raw * kr_raw, axis=-1, keepdims=True) * (1.0 / MLA_ROPE) + EPS)
    k_rope = (kr_raw * r_kr * krv_ref[...]) * cos + (kr_sw * r_kr * krvs_ref[...]) * sin
    kr_ref[...] = k_rope
    c_kv = _rms(z[:, 256:384], kvnw_ref[...])
    ckv_ref[...] = c_kv
    kv = _mm(_act(c_kv, hp), _wts(wkv_refs))
    v_ref[...] = kv[:, 512:].astype(v_ref.dtype)
    ks = []
    for h in range(MLA_HEADS):
        kraw = kv[:, LANES * h:LANES * (h + 1)]
        ks.append(_rms(kraw, kv_ref[...], MLA_NOPE) + k_rope)
    k_ref[...] = jnp.concatenate(ks, axis=1).astype(k_ref.dtype)
    c_q = _rms(z[:, 0:256], qnw_ref[...])
    qq = _mm(_act(c_q, hp), _wts(wq_refs))
    qs = []
    for h in range(MLA_HEADS):
        x = qq[:, LANES * h:LANES * (h + 1)]
        xs = qq[:, 512 + LANES * h:512 + LANES * (h + 1)]
        x2 = x * x
        r_rope = lax.rsqrt(jnp.sum(jnp.where(is_rope, x2, 0.0), axis=-1, keepdims=True) * (1.0 / MLA_ROPE) + EPS)
        r_nope = lax.rsqrt(jnp.sum(jnp.where(is_rope, 0.0, x2), axis=-1, keepdims=True) * (1.0 / MLA_NOPE) + EPS)
        y = x * jnp.where(is_rope, r_rope, r_nope) * qv_ref[...]
        ysw = xs * r_rope * qvs_ref[...]
        qs.append((y * cos + ysw * sin) * MLA_SCALE)
    q = jnp.concatenate(qs, axis=1)
    q_ref[...] = q.astype(q_ref.dtype)
    if absorb:
        kvec4 = jnp.concatenate([kv_ref[...]] * MLA_HEADS, axis=1)
        qabs_ref[...] = _dot((q * kvec4).astype(BF16), wabs_ref[...])


def mla_proj(z_b, cos_t, sin_t, pw, row0, n_rows, absorb, hp, tile=TOK_TILE):
    assert row0 % tile == 0 and n_rows % tile == 0
    t0 = row0 // tile
    rows = lambda w: pl.BlockSpec((tile, w), lambda i: (i + t0, 0))
    out_rows = lambda w: pl.BlockSpec((tile, w), lambda i: (i, 0))
    consts = [pw["q_norm_w"], pw["qvec"], pw["qvec_sw"], pw["kv_norm_w"], pw["kvec"], pw["krvec"], pw["krvec_sw"]]
    consts += pw["wq"] + pw["wkv"]
    if absorb:
        consts.append(pw["wabs"])
    qkv_dt = F32 if hp else BF16
    out_shape = [jax.ShapeDtypeStruct((n_rows, 512), qkv_dt)] * 3 + [jax.ShapeDtypeStruct((n_rows, LANES), F32)] * 2
    out_specs = [out_rows(512)] * 3 + [out_rows(LANES)] * 2
    if absorb:
        out_shape.append(jax.ShapeDtypeStruct((n_rows, 512), F32))
        out_specs.append(out_rows(512))
    return pl.pallas_call(
        functools.partial(_mla_proj_kernel, absorb=absorb, nw=len(pw["wq"])),
        grid=(n_rows // tile,),
        in_specs=[rows(640), rows(LANES), rows(LANES)] + [_full(a, 1) for a in consts],
        out_specs=out_specs,
        out_shape=out_shape,
        compiler_params=_cparams("parallel"),
        name="mla_proj_s" if absorb else "mla_proj_p",
    )(z_b, cos_t, sin_t, *consts)


def _flash_kernel(qi_ref, kj_ref, q_ref, k_ref, v_ref, o_ref, m_ref, l_ref, acc_ref, *, tq, hp):
    n = pl.program_id(1)
    qi, kj = qi_ref[n], kj_ref[n]

    @pl.when(kj == 0)
    def _():
        m_ref[...] = jnp.full_like(m_ref, -jnp.inf)
        l_ref[...] = jnp.zeros_like(l_ref)
        acc_ref[...] = jnp.zeros_like(acc_ref)

    def operand(ref, sl):
        x = ref[:, sl]
        return _split(x) if hp else (x,)

    def step(masked):
        for h in range(MLA_HEADS):
            sl = slice(LANES * h, LANES * (h + 1))
            s = _mm(operand(q_ref, sl), operand(k_ref, sl), nt=True)
            if masked:
                s = jnp.where(_row((tq, tq)) >= _lane((tq, tq)), s, -jnp.inf)
            m_prev = m_ref[h]
            m_new = jnp.maximum(m_prev, jnp.max(s, axis=-1, keepdims=True))
            alpha = jnp.exp(m_prev - m_new)
            p = jnp.exp(s - jnp.tile(m_new, (1, tq // LANES)))
            l_ref[h] = alpha * l_ref[h] + jnp.sum(p, axis=-1, keepdims=True)
            acc_ref[h] = alpha * acc_ref[h] + _mm(_act(p, hp), operand(v_ref, sl))
            m_ref[h] = m_new

    @pl.when(kj < qi)
    def _():
        step(False)

    @pl.when(kj == qi)
    def _():
        step(True)
        o_ref[...] = jnp.concatenate([acc_ref[h] / l_ref[h] for h in range(MLA_HEADS)], axis=1)


def flash_prompt(q, k, v, n_seq, seq_len, hp, tq=512):
    nq = seq_len // tq
    qi = np.array([i for i in range(nq) for _ in range(i + 1)], np.int32)
    kj = np.array([j for i in range(nq) for j in range(i + 1)], np.int32)
    grid_spec = pltpu.PrefetchScalarGridSpec(
        num_scalar_prefetch=2,
        grid=(n_seq, len(qi)),
        in_specs=[pl.BlockSpec((tq, 512), lambda b, n, qi, kj: (b * nq + qi[n], 0)),
                  pl.BlockSpec((tq, 512), lambda b, n, qi, kj: (b * nq + kj[n], 0)),
                  pl.BlockSpec((tq, 512), lambda b, n, qi, kj: (b * nq + kj[n], 0))],
        out_specs=pl.BlockSpec((tq, 512), lambda b, n, qi, kj: (b * nq + qi[n], 0)),
        scratch_shapes=[pltpu.VMEM((MLA_HEADS, tq, LANES), F32), pltpu.VMEM((MLA_HEADS, tq, LANES), F32),
                        pltpu.VMEM((MLA_HEADS, tq, LANES), F32)])
    return pl.pallas_call(
        functools.partial(_flash_kernel, tq=tq, hp=hp),
        grid_spec=grid_spec,
        out_shape=jax.ShapeDtypeStruct((n_seq * seq_len, 512), F32),
        compiler_params=_cparams("parallel", "arbitrary"),
        name="mla_flash_prompt",
    )(jnp.asarray(qi), jnp.asarray(kj), q, k, v)


def _mla_sample_kernel(pt_ref, cache_c, cache_r, qa_ref, qr_ref, cn_ref, rn_ref, wuk_ref, seg_ref, o_ref,
                       m_ref, l_ref, acc_ref, cbuf, rbuf, csem, rsem, *, npg, n_steps, layer):
    g = pl.program_id(1)
    n = pl.program_id(0) * n_steps + g
    n_total = pl.num_programs(0) * n_steps
    slot = n % 2

    def page_copies(step, s, p):
        page = pt_ref[step * npg + p]
        return (pltpu.make_async_copy(cache_c.at[layer, page], cbuf.at[s, p], csem.at[s]),
                pltpu.make_async_copy(cache_r.at[layer, page], rbuf.at[s, p], rsem.at[s]))

    def start_pages(step, s):
        for p in range(npg):
            for cp in page_copies(step, s, p):
                cp.start()

    @pl.when(n == 0)
    def _():
        start_pages(0, 0)

    @pl.when(n + 1 < n_total)
    def _():
        start_pages(n + 1, 1 - slot)

    for p in range(npg):
        for cp in page_copies(n, slot, p):
            cp.wait()

    @pl.when(g == 0)
    def _():
        m_ref[...] = jnp.full_like(m_ref, -jnp.inf)
        l_ref[...] = jnp.zeros_like(l_ref)
        acc_ref[...] = jnp.zeros_like(acc_ref)

    qa = qa_ref[...].astype(BF16)
    qr = qr_ref[...][:, :MLA_ROPE].astype(BF16)
    wuk = wuk_ref[...]
    seg = seg_ref[...]

    def nope_scores(cb):
        kraw = _dot(cb, wuk)
        ssq = _dot_nt(seg, (kraw * kraw).astype(BF16))
        return _dot_nt(qa, cb) * lax.rsqrt(ssq * (1.0 / MLA_NOPE) + EPS)

    def update(s_all, cb):
        m_prev = m_ref[...]
        m_new = jnp.maximum(m_prev, jnp.max(s_all, axis=-1, keepdims=True))
        alpha = jnp.exp(m_prev - m_new)
        p_all = jnp.exp(s_all - m_new)
        l_ref[...] = alpha * l_ref[...] + jnp.sum(p_all, axis=-1, keepdims=True)
        acc_ref[...] = alpha * acc_ref[...] + _dot(p_all.astype(BF16), cb)
        m_ref[...] = m_new

    cb_all = jnp.concatenate([cbuf[slot, p].astype(BF16) for p in range(npg)], axis=0)
    kr_all = jnp.concatenate([rbuf[slot, p].astype(BF16) for p in range(npg)], axis=1)
    update(nope_scores(cb_all) + _dot(qr, kr_all), cb_all)

    @pl.when(g == n_steps - 1)
    def _():
        cb = cn_ref[...].astype(BF16)
        s_new = nope_scores(cb) + _dot_nt(qr, rn_ref[...][:, :MLA_ROPE].astype(BF16))
        qtok = _row(s_new.shape) // MLA_HEADS
        s_new = jnp.where(_lane(s_new.shape) <= qtok, s_new, -jnp.inf)
        update(s_new, cb)
        o_ref[...] = acc_ref[...] / l_ref[...]


def mla_sample(page_table, cache_c, cache_rt, layer, qa, qr, c_new, r_new, wuk, seg):
    nb, n_pages = page_table.shape
    npg = PAGES_PER_STEP
    n_steps = n_pages // npg
    rows = qa.shape[1]
    s_pad = c_new.shape[1]

    per_seq = lambda r, w: pl.BlockSpec((None, r, w), lambda b, g, pt: (b, 0, 0))
    full = lambda a: pl.BlockSpec(a.shape, lambda b, g, pt: (0,) * a.ndim)
    any_spec = pl.BlockSpec(memory_space=pl.ANY)
    grid_spec = pltpu.PrefetchScalarGridSpec(
        num_scalar_prefetch=1,
        grid=(nb, n_steps),
        in_specs=[any_spec, any_spec,
                  per_seq(rows, LANES), per_seq(rows, LANES), per_seq(s_pad, LANES), per_seq(s_pad, LANES),
                  full(wuk), full(seg)],
        out_specs=per_seq(rows, LANES),
        scratch_shapes=[pltpu.VMEM((rows, 1), F32), pltpu.VMEM((rows, 1), F32), pltpu.VMEM((rows, LANES), F32),
                        pltpu.VMEM((2, npg, PAGE_SIZE, MLA_KV_LORA), F32),
                        pltpu.VMEM((2, npg, MLA_ROPE, PAGE_SIZE), F32),
                        pltpu.SemaphoreType.DMA((2,)), pltpu.SemaphoreType.DMA((2,))])
    return pl.pallas_call(
        functools.partial(_mla_sample_kernel, npg=npg, n_steps=n_steps, layer=layer),
        grid_spec=grid_spec,
        out_shape=jax.ShapeDtypeStruct((nb, rows, LANES), F32),
        compiler_params=_cparams("arbitrary", "arbitrary"),
        name="mla_sample",
    )(page_table.reshape(-1), cache_c, cache_rt, qa, qr, c_new, r_new, wuk, seg)


def _matmul_kernel(a_ref, w_ref, o_ref):
    o_ref[...] = _dot(a_ref[...].astype(BF16), w_ref[...])


def matmul(a, w, tile=TOK_TILE, name="matmul"):
    t, kdim = a.shape
    n = w.shape[1]
    return pl.pallas_call(
        _matmul_kernel,
        grid=(t // tile,),
        in_specs=[pl.BlockSpec((tile, kdim), lambda i: (i, 0)), pl.BlockSpec((kdim, n), lambda i: (0, 0))],
        out_specs=pl.BlockSpec((tile, n), lambda i: (i, 0)),
        out_shape=jax.ShapeDtypeStruct((t, n), F32),
        compiler_params=_cparams("parallel"),
        name=name,
    )(a, w)


def _mix_out_kernel(*refs, nw, n_prompt_tiles):
    p_refs, s_refs, x_ref, g_refs, n2_ref = refs[0:4], refs[4:8], refs[8], refs[9:13], refs[13]
    rest = refs[14:]
    w_groups = [rest[nw * g:nw * (g + 1)] for g in range(5)]
    x1_ref, q_ref = rest[5 * nw:]
    hp = nw == 2
    is_prompt = pl.program_id(0) < n_prompt_tiles
    acc = x_ref[...]
    for p_ref, s_ref, g_ref, w_refs in zip(p_refs, s_refs, g_refs, w_groups[:4]):
        o = jnp.where(is_prompt, p_ref[...], s_ref[...])
        acc = acc + _mm(_act(_rms(o, g_ref[...], GROUP_W), hp), _wts(w_refs))
    x1_ref[...] = acc
    q_ref[...] = _mm(_act(_rms(acc, n2_ref[...]), hp), _wts(w_groups[4]))


def mix_out(o_prompt, o_sample, x, gw, w_out, n2w, wq, tile=TOK_TILE):
    t = x.shape[0]
    npt = o_prompt[0].shape[0] // tile
    assert all(a.shape[0] == tile for a in o_sample) and t == (npt + 1) * tile
    prompt_rows = lambda a: pl.BlockSpec((tile, a.shape[1]), lambda i: (jnp.minimum(i, npt - 1), 0))
    acts = list(o_prompt) + list(o_sample) + [x]
    consts = list(gw) + [n2w.reshape(1, -1)] + [a for w in w_out for a in w] + list(wq)
    return pl.pallas_call(
        functools.partial(_mix_out_kernel, nw=len(wq), n_prompt_tiles=npt),
        grid=(t // tile,),
        in_specs=[prompt_rows(a) for a in o_prompt] + [_full(a, 1) for a in o_sample]
                 + [pl.BlockSpec((tile, D_MODEL), lambda i: (i, 0))] + [_full(a, 1) for a in consts],
        out_specs=[pl.BlockSpec((tile, D_MODEL), lambda i: (i, 0)), pl.BlockSpec((tile, CA_W), lambda i: (i, 0))],
        out_shape=[jax.ShapeDtypeStruct((t, D_MODEL), F32), jax.ShapeDtypeStruct((t, CA_W), F32)],
        compiler_params=_cparams("parallel"),
        name="mix_out",
    )(*acts, *consts)


def _cross_kernel(q_ref, mk_ref, mv_ref, x_ref, qn_ref, *rest, nw):
    wo_refs, (o_ref,) = rest[:nw], rest[nw:]
    hp = nw == 2
    q = q_ref[...]
    outs = []
    for h in range(CA_HEADS):
        sl = slice(CA_HD * h, CA_HD * (h + 1))
        qh = _rms(q[:, sl], qn_ref[...])
        s = _mm(_act(qh, hp), _act(mk_ref[:, sl], hp), nt=True) * (CA_HD ** -0.5)
        s = s - jnp.max(s, axis=-1, keepdims=True)
        e = jnp.exp(s)
        p = e / jnp.sum(e, axis=-1, keepdims=True)
        outs.append(_mm(_act(p, hp), _act(mv_ref[:, sl], hp)))
    o = jnp.concatenate(outs, axis=1)
    o_ref[...] = x_ref[...] + _mm(_act(o, hp), _wts(wo_refs))


def cross_attend(q, mem_k, mem_v, x, qnw, wo, n_seq, seq_len, tl):
    nt = seq_len // tl
    mem_spec = pl.BlockSpec((None, N_MEM, CA_W), lambda s, i: (s, 0, 0))
    return pl.pallas_call(
        functools.partial(_cross_kernel, nw=len(wo)),
        grid=(n_seq, nt),
        in_specs=[pl.BlockSpec((tl, CA_W), lambda s, i: (s * nt + i, 0)),
                  mem_spec, mem_spec,
                  pl.BlockSpec((tl, D_MODEL), lambda s, i: (s * nt + i, 0)),
                  pl.BlockSpec((1, CA_HD), lambda s, i: (0, 0))] + [_full(a, 2) for a in wo],
        out_specs=pl.BlockSpec((tl, D_MODEL), lambda s, i: (s * nt + i, 0)),
        out_shape=jax.ShapeDtypeStruct((n_seq * seq_len, D_MODEL), F32),
        compiler_params=_cparams("parallel", "parallel"),
        name="cross_attend",
    )(q, mem_k, mem_v, x, qnw.reshape(1, CA_HD), *wo)


SAMPLE_SEQS_PER_STEP = 8


def _cross_sample_kernel(q_ref, mk_ref, mv_ref, x_ref, qn_ref, *rest, nw, s_len):
    wo_refs, (o_ref,) = rest[:nw], rest[nw:]
    hp = nw == 2
    q = q_ref[...]
    rows = CA_HEADS * s_len
    shape = (rows, N_MEM * CA_HEADS)
    own_head = (_lane(shape) % CA_HEADS) == (_row(shape) // s_len)
    outs = []
    for s in range(SAMPLE_SEQS_PER_STEP):
        qs = q[s * s_len:(s + 1) * s_len]
        qh = jnp.concatenate([qs[:, CA_HD * h:CA_HD * (h + 1)] for h in range(CA_HEADS)], axis=0)
        qn = _rms(qh, qn_ref[...])
        sc = _mm(_act(qn, hp), _act(mk_ref[s], hp), nt=True) * (CA_HD ** -0.5)
        sc = jnp.where(own_head, sc, -jnp.inf)
        sc = sc - jnp.max(sc, axis=-1, keepdims=True)
        e = jnp.exp(sc)
        p = e / jnp.sum(e, axis=-1, keepdims=True)
        o = _mm(_act(p, hp), _act(mv_ref[s], hp))
        outs.append(jnp.concatenate([o[h * s_len:(h + 1) * s_len] for h in range(CA_HEADS)], axis=1))
    o_all = jnp.concatenate(outs, axis=0)
    o_ref[...] = x_ref[...] + _mm(_act(o_all, hp), _wts(wo_refs))


def cross_attend_sample(q, mem_k, mem_v, x, qnw, wo, s_len, layer):
    sb = SAMPLE_SEQS_PER_STEP
    n_seq = q.shape[0] // s_len
    rows = sb * s_len
    mem_spec = pl.BlockSpec((None, sb, N_MEM * CA_HEADS, CA_HD), lambda i: (layer, i, 0, 0))
    return pl.pallas_call(
        functools.partial(_cross_sample_kernel, nw=len(wo), s_len=s_len),
        grid=(n_seq // sb,),
        in_specs=[pl.BlockSpec((rows, CA_W), lambda i: (i, 0)), mem_spec, mem_spec,
                  pl.BlockSpec((rows, D_MODEL), lambda i: (i, 0)),
                  pl.BlockSpec((1, CA_HD), lambda i: (0, 0))] + [_full(a, 1) for a in wo],
        out_specs=pl.BlockSpec((rows, D_MODEL), lambda i: (i, 0)),
        out_shape=jax.ShapeDtypeStruct((n_seq * s_len, D_MODEL), F32),
        compiler_params=_cparams("parallel"),
        name="cross_attend_sample",
    )(q, mem_k, mem_v, x, qnw.reshape(1, CA_HD), *wo)


def _swiglu_into(h, wg_refs, wu_refs, wd_refs, o_ref, width):
    hp = len(wg_refs) == 2
    for c0 in range(0, width, FF_CHUNK):
        cols = (slice(None), slice(c0, c0 + FF_CHUNK))
        g = _mm(h, _wts(wg_refs, cols))
        u = _mm(h, _wts(wu_refs, cols))
        a = _act(jax.nn.silu(g) * u, hp)
        o_ref[...] += _mm(a, _wts(wd_refs, (slice(c0, c0 + FF_CHUNK), slice(None))))


def _ffn_kernel(x_ref, nw_ref, *rest, nw, width):
    wg, wu, wd, (o_ref,) = rest[:nw], rest[nw:2 * nw], rest[2 * nw:3 * nw], rest[3 * nw:]
    x = x_ref[...]

    @pl.when(pl.program_id(1) == 0)
    def _():
        o_ref[...] = x

    _swiglu_into(_act(_rms(x, nw_ref[...]), nw == 2), wg, wu, wd, o_ref, width)


def ffn_dense(x, nw, wg, wu, wd, ff_block, tile=TOK_TILE):
    t = x.shape[0]
    up = lambda: pl.BlockSpec((D_MODEL, ff_block), lambda i, j: (0, j))
    down = lambda: pl.BlockSpec((ff_block, D_MODEL), lambda i, j: (j, 0))
    n = len(wg)
    return pl.pallas_call(
        functools.partial(_ffn_kernel, nw=n, width=ff_block),
        grid=(t // tile, D_FF // ff_block),
        in_specs=[pl.BlockSpec((tile, D_MODEL), lambda i, j: (i, 0)), pl.BlockSpec((1, D_MODEL), lambda i, j: (0, 0))]
                 + [up() for _ in range(2 * n)] + [down() for _ in range(n)],
        out_specs=pl.BlockSpec((tile, D_MODEL), lambda i, j: (i, 0)),
        out_shape=jax.ShapeDtypeStruct((t, D_MODEL), F32),
        compiler_params=_cparams("parallel", "arbitrary"),
        name="ffn_dense",
    )(x, nw.reshape(1, -1), *wg, *wu, *wd)


def _router_kernel(x_ref, nw_ref, r_ref, h_ref, g_ref):
    h = _rms(x_ref[...], nw_ref[...])
    h_ref[...] = h
    logits = jnp.dot(h, r_ref[...], preferred_element_type=F32, precision=lax.Precision.HIGHEST)
    lane = _lane(logits.shape)
    neg = -jnp.inf
    l1 = jnp.where(lane < N_EXPERTS, logits, neg)
    m1 = jnp.max(l1, axis=-1, keepdims=True)
    i1 = jnp.min(jnp.where(l1 == m1, lane, LANES), axis=-1, keepdims=True)
    l2 = jnp.where(lane == i1, neg, l1)
    m2 = jnp.max(l2, axis=-1, keepdims=True)
    i2 = jnp.min(jnp.where(l2 == m2, lane, LANES), axis=-1, keepdims=True)
    e2 = jnp.exp(m2 - m1)
    den = 1.0 + e2
    gates = jnp.where(lane == i1, 1.0 / den, jnp.where(lane == i2, e2 / den, 0.0))
    g_ref[...] = jnp.where(lane == N_EXPERTS, i1.astype(F32), jnp.where(lane == N_EXPERTS + 1, i2.astype(F32), gates))


def moe_router(x, nw, router_pad, tile=TOK_TILE):
    t = x.shape[0]
    return pl.pallas_call(
        _router_kernel,
        grid=(t // tile,),
        in_specs=[pl.BlockSpec((tile, D_MODEL), lambda i: (i, 0)), pl.BlockSpec((1, D_MODEL), lambda i: (0, 0)),
                  pl.BlockSpec((D_MODEL, LANES), lambda i: (0, 0))],
        out_specs=[pl.BlockSpec((tile, D_MODEL), lambda i: (i, 0)), pl.BlockSpec((tile, LANES), lambda i: (i, 0))],
        out_shape=[jax.ShapeDtypeStruct((t, D_MODEL), F32), jax.ShapeDtypeStruct((t, LANES), F32)],
        compiler_params=_cparams("parallel"),
        name="moe_router",
    )(x, nw.reshape(1, -1), router_pad)


def _experts_kernel(be_ref, nv_ref, src_ref, nxt_ref, h_hbm, wg_ref, wu_ref, wd_ref, o_ref, xbuf, sem, *, tile):
    b = pl.program_id(0)
    slot = b % 2
    nv = nv_ref[0]

    def row_copy(idx_ref, i, s):
        return pltpu.make_async_copy(h_hbm.at[pl.ds(idx_ref[0, i], 1), :], xbuf.at[s, pl.ds(i, 1), :], sem.at[s])

    def start_rows(idx_ref, s):
        def body(i, c):
            row_copy(idx_ref, i, s).start()
            return c
        lax.fori_loop(0, tile, body, 0, unroll=8)

    @pl.when((b == 0) & (nv > 0))
    def _():
        start_rows(src_ref, 0)

    @pl.when(b + 1 < nv)
    def _():
        start_rows(nxt_ref, 1 - slot)

    o_ref[...] = jnp.zeros_like(o_ref)

    @pl.when(b < nv)
    def _():
        def body(i, c):
            row_copy(src_ref, i, slot).wait()
            return c
        lax.fori_loop(0, tile, body, 0, unroll=8)
        _swiglu_into((xbuf[slot].astype(BF16),), (wg_ref,), (wu_ref,), (wd_ref,), o_ref, D_FF)


def moe_experts(h, src, blk_e, n_valid, wg, wu, wd, j, tile=MOE_TILE):
    n_blk = src.shape[0]
    wspec = lambda a: pl.BlockSpec((None, None) + a.shape[2:], lambda b, be, nv: (j, be[b], 0, 0))
    idx_spec = lambda f: pl.BlockSpec((None, 1, tile), f, memory_space=pltpu.SMEM)
    grid_spec = pltpu.PrefetchScalarGridSpec(
        num_scalar_prefetch=2,
        grid=(n_blk,),
        in_specs=[idx_spec(lambda b, be, nv: (b, 0, 0)),
                  idx_spec(lambda b, be, nv: (jnp.minimum(b + 1, n_blk - 1), 0, 0)),
                  pl.BlockSpec(memory_space=pl.ANY), wspec(wg), wspec(wu), wspec(wd)],
        out_specs=pl.BlockSpec((tile, D_MODEL), lambda b, be, nv: (b, 0)),
        scratch_shapes=[pltpu.VMEM((2, tile, D_MODEL), F32), pltpu.SemaphoreType.DMA((2,))])
    return pl.pallas_call(
        functools.partial(_experts_kernel, tile=tile),
        grid_spec=grid_spec,
        out_shape=jax.ShapeDtypeStruct((n_blk * tile, D_MODEL), F32),
        compiler_params=_cparams("arbitrary"),
        name="moe_experts",
    )(blk_e, n_valid, src, src, h, wg, wu, wd)


def moe_ffn(x, nw, router_pad, wg, wu, wd, j):
    t = x.shape[0]
    h, gfull = moe_router(x, nw, router_pad)
    e_idx = gfull[:, N_EXPERTS:N_EXPERTS + 2].astype(jnp.int32)
    g2 = jnp.take_along_axis(gfull[:, :N_EXPERTS], e_idx, axis=1)
    sel = (jnp.arange(N_EXPERTS, dtype=jnp.int32)[None, :] == e_idx[:, 0:1]) | (
        jnp.arange(N_EXPERTS, dtype=jnp.int32)[None, :] == e_idx[:, 1:2])
    pos = jnp.cumsum(sel.astype(jnp.int32), axis=0) - 1
    counts = pos[-1] + 1
    nb_e = (counts + MOE_TILE - 1) // MOE_TILE
    blk_end = jnp.cumsum(nb_e)
    row_start = (blk_end - nb_e) * MOE_TILE
    slot = row_start[None, :] + pos
    assert (2 * t) % MOE_TILE == 0
    n_blk = (2 * t + N_EXPERTS * MOE_TILE) // MOE_TILE
    stride = t + MOE_TILE
    big = N_EXPERTS * stride
    tok = jnp.arange(t, dtype=jnp.int32)[:, None]
    real = e_idx * stride + tok
    pad_e = nb_e * MOE_TILE - counts
    fill = jnp.arange(MOE_TILE, dtype=jnp.int32)[None, :]
    dummy = jnp.where(fill < pad_e[:, None],
                      jnp.arange(N_EXPERTS, dtype=jnp.int32)[:, None] * stride + t + fill, big)
    keys = jnp.sort(jnp.concatenate([real.reshape(-1), dummy.reshape(-1)]))
    src = keys % stride
    src = jnp.where((src >= t) | (keys >= big), 0, src).reshape(n_blk, 1, MOE_TILE)
    blk_e = jnp.minimum(jnp.searchsorted(blk_end, jnp.arange(n_blk, dtype=jnp.int32), side="right"),
                        N_EXPERTS - 1).astype(jnp.int32)
    y_buf = moe_experts(h, src, blk_e, blk_end[-1:].astype(jnp.int32), wg, wu, wd, j)
    s2 = jnp.take_along_axis(slot, e_idx, axis=1)
    y = y_buf[s2[:, 0]] * g2[:, 0:1] + y_buf[s2[:, 1]] * g2[:, 1:2]
    return x + y


def _zeros(r, c):
    return jnp.zeros((r, c), F32)


def _prep_w_in(w, hp):
    d = w.shape[0]
    k_r = w[:, 640:672]
    kr_slot = jnp.concatenate([k_r, _zeros(d, 96)], axis=1)
    kr_sw = jnp.concatenate([k_r[:, 16:], k_r[:, :16], _zeros(d, 96)], axis=1)
    return _w(jnp.concatenate([w[:, 0:256], w[:, 256:640], kr_slot, kr_sw, w[:, 672:2208], w[:, 2208:2720]],
                              axis=1), hp)


W_IN_SPLITS = (256, 640, 1536, 512)


def _slot_vec(rope_w, nope_w):
    z = jnp.zeros((32,), F32)
    return jnp.concatenate([rope_w, nope_w, z]).reshape(1, LANES)


def _prep_mla(lw, hp):
    wq, wkv = lw["mla_w_uq"], lw["mla_w_ukv"]
    slots, sw = [], []
    for h in range(MLA_HEADS):
        nope = wq[:, 96 * h:96 * h + 64]
        rope = wq[:, 96 * h + 64:96 * h + 96]
        slots.append(jnp.concatenate([rope, nope, _zeros(MLA_Q_LORA, 32)], axis=1))
        sw.append(jnp.concatenate([rope[:, 16:], rope[:, :16], _zeros(MLA_Q_LORA, 96)], axis=1))
    kslots = [jnp.concatenate([_zeros(128, 32), wkv[:, 128 * h:128 * h + 64], _zeros(128, 32)], axis=1)
              for h in range(MLA_HEADS)]
    vslots = [jnp.concatenate([wkv[:, 128 * h + 64:128 * h + 128], _zeros(128, 64)], axis=1)
              for h in range(MLA_HEADS)]
    z96 = jnp.zeros((96,), F32)
    qr, kr = lw["mla_qn_rope_w"], lw["mla_kn_rope_w"]
    wabs = jnp.zeros((512, 512), F32)
    for h in range(MLA_HEADS):
        wabs = wabs.at[128 * h + 32:128 * h + 96, 128 * h:128 * h + 128].set(wkv[:, 128 * h:128 * h + 64].T)
    wuv_bd = jnp.zeros((512, 256), F32)
    for h in range(MLA_HEADS):
        wuv_bd = wuv_bd.at[128 * h:128 * h + 128, 64 * h:64 * h + 64].set(wkv[:, 128 * h + 64:128 * h + 128])
    return dict(
        q_norm_w=lw["mla_q_norm_w"].reshape(1, -1),
        wq=_w(jnp.concatenate(slots + sw, axis=1), hp),
        qvec=_slot_vec(qr, lw["mla_qn_nope_w"]),
        qvec_sw=jnp.concatenate([qr[16:], qr[:16], z96]).reshape(1, LANES),
        kv_norm_w=lw["mla_kv_norm_w"].reshape(1, -1),
        wkv=_w(jnp.concatenate(kslots + vslots, axis=1), hp),
        kvec=_slot_vec(jnp.zeros((32,), F32), lw["mla_kn_nope_w"]),
        krvec=jnp.concatenate([kr, z96]).reshape(1, LANES),
        krvec_sw=jnp.concatenate([kr[16:], kr[:16], z96]).reshape(1, LANES),
        wabs=wabs.astype(BF16),
        wuk=jnp.concatenate([wkv[:, 128 * h:128 * h + 64] for h in range(MLA_HEADS)], axis=1).astype(BF16),
        wuv_bd=wuv_bd.astype(BF16),
    )


def _rope_tables(pos):
    half = MLA_ROPE // 2
    inv = 1.0 / (ROPE_THETA ** (jnp.arange(half, dtype=F32) / half))
    ang = pos.astype(F32)[:, None] * inv[None, :]
    cos, sin = jnp.cos(ang), jnp.sin(ang)
    n = pos.shape[0]
    cos_t = jnp.concatenate([cos, cos, jnp.ones((n, 64), F32), jnp.zeros((n, 32), F32)], axis=1)
    sin_t = jnp.concatenate([-sin, sin, jnp.zeros((n, 96), F32)], axis=1)
    return cos_t, sin_t


def _pad_new(a):
    n, s, w = a.shape
    return jnp.concatenate([a, jnp.zeros((n, 8 - s, w), a.dtype)], axis=1)


def _pad_rows(w, rows_per, pad_to):
    g = w.shape[0] // rows_per
    w = w.reshape(g, rows_per, -1)
    return jnp.concatenate([w, jnp.zeros((g, pad_to - rows_per, w.shape[-1]), w.dtype)], axis=1).reshape(
        g * pad_to, -1)


def _sample_pool(u, prefix, w_bd, scale):
    n, L, c = u.shape
    ext = jnp.concatenate([prefix, u], axis=1)
    cs = jnp.concatenate([jnp.zeros((n, 1, c), F32), jnp.cumsum(ext, axis=1)], axis=1)
    hi = cs[:, POOL_PAD + 1:]
    pooled = []
    for g, w in enumerate(POOL_WINDOWS):
        sl = slice(64 * g, 64 * (g + 1))
        lo = cs[:, POOL_PAD + 1 - w:POOL_PAD + 1 - w + L, sl]
        pooled.append((hi[..., sl] - lo) / float(w))
    d = jnp.concatenate(pooled, axis=-1) - u
    y = jnp.dot(d.reshape(n * L, c).astype(BF16), w_bd, preferred_element_type=F32) * scale
    return y, ext[:, -POOL_PAD:]


def _sample_hgrn(zc, s0, lb, onw):
    n, L, _ = zc.shape
    q = zc[..., 0:512].reshape(n, L, HG_HEADS, HG_DK)
    zf = zc[..., 512:1024].reshape(n, L, HG_HEADS, HG_DK)
    v = zc[..., 1024:1280].reshape(n, L, HG_HEADS, HG_DV)
    g = zc[..., 1280:1536].reshape(n, L, HG_HEADS, HG_DV)
    lbh = lb.reshape(HG_HEADS, HG_DK)
    f = lbh + (1.0 - lbh) * jax.nn.sigmoid(zf)
    k = (1.0 - lbh) * jax.nn.sigmoid(-zf)
    s = s0
    outs = []
    for t in range(L):
        s = f[:, t][..., None] * s + k[:, t][..., None] * v[:, t][:, :, None, :]
        outs.append(jnp.sum(s * q[:, t][..., None], axis=2))
    o = jnp.stack(outs, axis=1)
    o = _rms(o, onw, HG_DV) * jax.nn.silu(g)
    return o.reshape(n * L, GROUP_W), s


def _sample_conv(u, prefix, cw, cb, lw, lb, pw, pb):
    n, L, _ = u.shape
    g = u[..., :CONV_C] * jax.nn.sigmoid(u[..., CONV_C:])
    ext = jnp.concatenate([prefix, g], axis=1)
    y = sum(ext[:, j:j + L, :] * cw[j][None, None, :] for j in range(CONV_W)) + cb
    mu = jnp.mean(y, axis=-1, keepdims=True)
    yc = y - mu
    var = jnp.mean(yc * yc, axis=-1, keepdims=True)
    y = yc * lax.rsqrt(var + EPS) * lw + lb
    y = jnp.dot(jax.nn.silu(y).reshape(n * L, CONV_C).astype(BF16), pw, preferred_element_type=F32) + pb
    return y, ext[:, -CONV_PAD:]


def kernel(x_prompt, x_sample, cache_kv_latent, cache_k_rope, cache_mem_k, cache_mem_v, state_pool, state_hgrn, state_conv, page_table, mem_prompt, norm1_w, w_in, pool_w, pool_scale, mla_q_norm_w, mla_w_uq, mla_kv_norm_w, mla_w_ukv, mla_qn_nope_w, mla_qn_rope_w, mla_kn_nope_w, mla_kn_rope_w, hg_lb_param, hg_onorm_w, conv_w, conv_b, conv_ln_w, conv_ln_b, conv_pw_w, conv_pw_b, grp_norm_w, w_out, norm2_w, mem_norm_w, ca_wq, ca_wk, ca_wv, ca_qn_w, ca_kn_w, ca_wo, norm3_w, ffn_w_gate, ffn_w_up, ffn_w_down, moe_router, moe_w_gate, moe_w_up, moe_w_down):
    bp, seq, d = x_prompt.shape
    nb, s_len, _ = x_sample.shape
    depth = w_in.shape[0]
    tp, ts = bp * seq, nb * s_len
    n_past = page_table.shape[1] * PAGE_SIZE
    n_mem = mem_prompt.shape[1]
    first_moe = 1

    sm = jax.nn.softmax(hg_lb_param.astype(F32), axis=0)
    lower_bounds = jnp.cumsum(sm, axis=0) - sm[:1]

    pos = jnp.concatenate([jnp.tile(jnp.arange(seq, dtype=jnp.int32), bp),
                           jnp.tile(n_past + jnp.arange(s_len, dtype=jnp.int32), nb)])
    cos_t, sin_t = _rope_tables(pos)
    seg = (jnp.arange(256)[None, :] // 64 == jnp.arange(MLA_HEADS * s_len)[:, None] % MLA_HEADS).astype(BF16)
    cache_rt = jnp.swapaxes(cache_k_rope, 2, 3)
    moe_wg, moe_wu, moe_wd = moe_w_gate.astype(BF16), moe_w_up.astype(BF16), moe_w_down.astype(BF16)
    mem_k_all = cache_mem_k.reshape(depth, nb, n_mem * CA_HEADS, CA_HD)
    mem_v_all = cache_mem_v.reshape(depth, nb, n_mem * CA_HEADS, CA_HD)

    x = jnp.concatenate([x_prompt.reshape(tp, d), x_sample.reshape(ts, d)], axis=0)
    mem2d = mem_prompt.reshape(bp * n_mem, d)
    outs = [[] for _ in range(12)]

    for l in range(depth):
        hp = l <= first_moe
        lw = dict(mla_q_norm_w=mla_q_norm_w[l], mla_w_uq=mla_w_uq[l], mla_kv_norm_w=mla_kv_norm_w[l],
                  mla_w_ukv=mla_w_ukv[l], mla_qn_nope_w=mla_qn_nope_w[l], mla_qn_rope_w=mla_qn_rope_w[l],
                  mla_kn_nope_w=mla_kn_nope_w[l], mla_kn_rope_w=mla_kn_rope_w[l])
        pw = _prep_mla(lw, hp)
        z_a, z_b, z_c, z_d = norm_matmul(x, norm1_w[l], _prep_w_in(w_in[l], hp), W_IN_SPLITS, name="norm_w_in")

        w_bd = _w(jax.scipy.linalg.block_diag(*[pool_w[l, g] for g in range(4)]), hp)
        o_a_p, pool_st = pool_prompt(z_a, w_bd, pool_scale[l], bp, seq)
        o_a_s, pool_st_s = _sample_pool(z_a[tp:].reshape(nb, s_len, GROUP_W), state_pool[l], w_bd[0], pool_scale[l])
        outs[0].append(pool_st[:, 1:])
        outs[7].append(pool_st_s)

        q_p, k_p, v_p, ckv_p, kr_p = mla_proj(z_b, cos_t, sin_t, pw, 0, tp, absorb=False, hp=hp)
        o_b_p = flash_prompt(q_p, k_p, v_p, bp, seq, hp)
        q_s, _, _, ckv_s, kr_s, qabs_s = mla_proj(z_b, cos_t, sin_t, pw, tp, ts, absorb=True, hp=hp)
        rows = MLA_HEADS * s_len
        o_lat = mla_sample(page_table, cache_kv_latent, cache_rt, l,
                           qabs_s.reshape(nb, rows, LANES), q_s.astype(F32).reshape(nb, rows, LANES),
                           _pad_new(ckv_s.reshape(nb, s_len, LANES)), _pad_new(kr_s.reshape(nb, s_len, LANES)),
                           pw["wuk"], seg)
        o_b_s = matmul(o_lat.reshape(ts, 512), pw["wuv_bd"], name="mla_v_up")
        o_b_s = jnp.concatenate([o_b_s.reshape(ts, 4, 64), jnp.zeros((ts, 4, 64), F32)], axis=-1).reshape(ts, 512)
        outs[1].append(ckv_p.reshape(bp, seq, MLA_KV_LORA))
        outs[2].append(kr_p[:, :MLA_ROPE].reshape(bp, seq, MLA_ROPE))
        outs[8].append(ckv_s.reshape(nb, s_len, MLA_KV_LORA))
        outs[9].append(kr_s[:, :MLA_ROPE].reshape(nb, s_len, MLA_ROPE))

        onw_slot = jnp.concatenate([hg_onorm_w[l], jnp.zeros((64,), F32)])
        o_c_p, hg_st = hgrn_prompt(z_c, lower_bounds[l], onw_slot, bp, seq, hp)
        o_c_s, hg_st_s = _sample_hgrn(z_c[tp:].reshape(nb, s_len, -1), state_hgrn[l], lower_bounds[l], hg_onorm_w[l])
        outs[3].append(hg_st[..., :HG_DV])
        outs[10].append(hg_st_s)

        cw_pad = jnp.concatenate([conv_w[l], jnp.zeros((1, CONV_C), F32)], axis=0)
        pw_c = _w(conv_pw_w[l], hp)
        o_d_p, conv_st = conv_prompt(z_d, cw_pad, conv_b[l], conv_ln_w[l], conv_ln_b[l], pw_c, conv_pw_b[l], bp, seq)
        o_d_s, conv_st_s = _sample_conv(z_d[tp:].reshape(nb, s_len, -1), state_conv[l], conv_w[l], conv_b[l],
                                        conv_ln_w[l], conv_ln_b[l], pw_c[0], conv_pw_b[l])
        outs[4].append(conv_st[:, 2:])
        outs[11].append(conv_st_s)

        gnw = grp_norm_w[l]
        gw = [gnw[0:256].reshape(1, -1), _pad_rows(gnw[256:512].reshape(-1, 1), 64, 128).reshape(1, -1),
              gnw[512:768].reshape(1, -1), gnw[768:1024].reshape(1, -1)]
        wo_l = w_out[l]
        w_out_parts = [_w(wo_l[0:256], hp), _w(_pad_rows(wo_l[256:512], 64, 128), hp),
                       _w(wo_l[512:768], hp), _w(wo_l[768:1024], hp)]
        x1, q_ca = mix_out((o_a_p, o_b_p, o_c_p, o_d_p), (o_a_s, o_b_s, o_c_s, o_d_s), x, gw, w_out_parts,
                           norm2_w[l], _w(ca_wq[l], hp))

        w_kv = _w(jnp.concatenate([ca_wk[l], ca_wv[l]], axis=1), hp)
        mk_p, mv_p = norm_matmul(mem2d, mem_norm_w[l], w_kv, (CA_W, CA_W), slot_norm_w=ca_kn_w[l],
                                 tile=min(TOK_TILE, mem2d.shape[0]), name="memory_kv")
        outs[5].append(mk_p.reshape(bp, n_mem, CA_HEADS, CA_HD))
        outs[6].append(mv_p.reshape(bp, n_mem, CA_HEADS, CA_HD))

        wo_ca = _w(ca_wo[l], hp)
        x2_p = cross_attend(q_ca, mk_p.reshape(bp, n_mem, CA_W), mv_p.reshape(bp, n_mem, CA_W), x1,
                            ca_qn_w[l], wo_ca, bp, seq, tl=min(512, seq))
        x2_s = cross_attend_sample(q_ca[tp:], mem_k_all, mem_v_all, x1[tp:], ca_qn_w[l], wo_ca, s_len, l)
        x2 = jnp.concatenate([x2_p, x2_s], axis=0)

        j = l // 2
        if l % 2 == 0:
            x = ffn_dense(x2, norm3_w[l], _w(ffn_w_gate[j], hp), _w(ffn_w_up[j], hp), _w(ffn_w_down[j], hp),
                          ff_block=FF_CHUNK if hp else D_FF)
        else:
            router_pad = jnp.concatenate([moe_router[j], jnp.zeros((d, LANES - N_EXPERTS), F32)], axis=1)
            x = moe_ffn(x2, norm3_w[l], router_pad, moe_wg, moe_wu, moe_wd, j)

    st = lambda k: jnp.stack(outs[k])
    return (x[:tp].reshape(bp, seq, d), x[tp:].reshape(nb, s_len, d),
            st(0), st(1), st(2), st(3), st(4), st(5), st(6), st(7), st(8), st(9), st(10), st(11))
```

```python
import functools

import numpy as np
import jax
import jax.numpy as jnp
from jax import lax
from jax.experimental import pallas as pl
from jax.experimental.pallas import tpu as pltpu

F32 = jnp.float32
BF16 = jnp.bfloat16
EPS = 1e-6

D_MODEL = 1024
GROUP_W = 256
POOL_WINDOWS = (2, 4, 8, 16)
POOL_PAD = 15
MLA_HEADS = 4
MLA_NOPE = 64
MLA_ROPE = 32
MLA_V = 64
MLA_Q_LORA = 256
MLA_KV_LORA = 128
MLA_SCALE = (MLA_NOPE + MLA_ROPE) ** -0.5
ROPE_THETA = 10000.0
PAGE_SIZE = 128
HG_HEADS = 4
HG_DK = 128
HG_DV = 64
CONV_W = 31
CONV_PAD = 30
CONV_C = 256
N_MEM = 256
CA_HEADS = 4
CA_HD = 128
CA_W = 512
D_FF = 2816
N_EXPERTS = 8

LANES = 128
VMEM_LIMIT_BYTES = 56 * 1024 * 1024
TOK_TILE = 512
FF_CHUNK = 256
MOE_TILE = 512
HG_CHUNK = 64
HG_SUB = 16
PAGES_PER_STEP = 16


def _cparams(*sem):
    return pltpu.CompilerParams(dimension_semantics=sem, vmem_limit_bytes=VMEM_LIMIT_BYTES)


def _rms(x, w, n=None):
    n = x.shape[-1] if n is None else n
    r = lax.rsqrt(jnp.sum(x * x, axis=-1, keepdims=True) * (1.0 / n) + EPS)
    return x * r * w


def _dot(a, b):
    return jnp.dot(a, b, preferred_element_type=F32)


def _dot_nt(a, b):
    return lax.dot_general(a, b, (((1,), (1,)), ((), ())), preferred_element_type=F32)


def _lane(shape):
    return lax.broadcasted_iota(jnp.int32, shape, len(shape) - 1)


def _row(shape):
    return lax.broadcasted_iota(jnp.int32, shape, len(shape) - 2)


def _split(a):
    hi = a.astype(BF16)
    return hi, (a - hi.astype(F32)).astype(BF16)


def _act(a, hp):
    return _split(a) if hp else (a.astype(BF16),)


def _wts(refs, idx=None):
    return tuple(r[...] if idx is None else r[idx] for r in refs)


def _mm(a, b, nt=False):
    d = _dot_nt if nt else _dot
    out = d(a[0], b[0])
    if len(a) > 1:
        out = out + d(a[1], b[0]) + d(a[0], b[1])
    return out


def _w(w, hp):
    return list(_split(w)) if hp else [w.astype(BF16)]


def _full(a, nargs):
    zeros = (0,) * a.ndim
    if nargs == 1:
        return pl.BlockSpec(a.shape, lambda i: zeros)
    if nargs == 2:
        return pl.BlockSpec(a.shape, lambda i, j: zeros)
    return pl.BlockSpec(a.shape, lambda i, j, k: zeros)


def _norm_matmul_kernel(x_ref, nw_ref, *rest, splits, slot_norm, nw):
    w_refs, rest = rest[:nw], rest[nw:]
    if slot_norm:
        sw_ref, out_refs = rest[0], rest[1:]
    else:
        out_refs = rest
    h = _act(_rms(x_ref[...], nw_ref[...]), nw == 2)
    off = 0
    for idx, (o_ref, n) in enumerate(zip(out_refs, splits)):
        y = _mm(h, _wts(w_refs, (slice(None), slice(off, off + n))))
        if slot_norm and idx == 0:
            y = jnp.concatenate(
                [_rms(y[:, s:s + LANES], sw_ref[...]) for s in range(0, n, LANES)], axis=1)
        o_ref[...] = y
        off += n


def norm_matmul(x, nw, w, splits, slot_norm_w=None, tile=TOK_TILE, name="norm_matmul"):
    t, d = x.shape
    n = w[0].shape[1]
    assert sum(splits) == n and t % tile == 0
    in_specs = [pl.BlockSpec((tile, d), lambda i: (i, 0)), pl.BlockSpec((1, d), lambda i: (0, 0))]
    in_specs += [_full(a, 1) for a in w]
    args = [x, nw.reshape(1, d)] + list(w)
    if slot_norm_w is not None:
        in_specs.append(pl.BlockSpec((1, LANES), lambda i: (0, 0)))
        args.append(slot_norm_w.reshape(1, LANES))
    return pl.pallas_call(
        functools.partial(_norm_matmul_kernel, splits=splits, slot_norm=slot_norm_w is not None, nw=len(w)),
        grid=(t // tile,),
        in_specs=in_specs,
        out_specs=[pl.BlockSpec((tile, s), lambda i: (i, 0)) for s in splits],
        out_shape=[jax.ShapeDtypeStruct((t, s), F32) for s in splits],
        compiler_params=_cparams("parallel"),
        name=name,
    )(*args)


def _pool_kernel(u_ref, sc_ref, *rest, tl, nw):
    w_refs, (o_ref, st_ref, hist_ref) = rest[:nw], rest[nw:]
    i = pl.program_id(1)

    @pl.when(i == 0)
    def _():
        hist_ref[...] = jnp.zeros_like(hist_ref)

    u = u_ref[...]
    ext = jnp.concatenate([hist_ref[...], u], axis=0)
    s2 = ext + pltpu.roll(ext, 1, 0)
    s4 = s2 + pltpu.roll(s2, 2, 0)
    s8 = s4 + pltpu.roll(s4, 4, 0)
    s16 = s8 + pltpu.roll(s8, 8, 0)
    lane = _lane((tl, GROUP_W))
    grp = lane // 64
    pooled = jnp.where(grp == 0, s2[16:], jnp.where(grp == 1, s4[16:], jnp.where(grp == 2, s8[16:], s16[16:])))
    win = jnp.where(grp == 0, 2, jnp.where(grp == 1, 4, jnp.where(grp == 2, 8, 16)))
    pos = i * tl + _row((tl, GROUP_W))
    cnt = jnp.minimum(win, pos + 1).astype(F32)
    d = pooled / cnt - u
    o_ref[...] = _mm(_act(d, nw == 2), _wts(w_refs)) * sc_ref[...]
    hist_ref[...] = ext[tl:]
    st_ref[0] = ext[tl:]


def pool_prompt(z_a, w_bd, scale, n_seq, seq_len, tl=512):
    nt = seq_len // tl
    return pl.pallas_call(
        functools.partial(_pool_kernel, tl=tl, nw=len(w_bd)),
        grid=(n_seq, nt),
        in_specs=[pl.BlockSpec((tl, GROUP_W), lambda s, i: (s * nt + i, 0)),
                  pl.BlockSpec((1, GROUP_W), lambda s, i: (0, 0))] + [_full(a, 2) for a in w_bd],
        out_specs=[pl.BlockSpec((tl, GROUP_W), lambda s, i: (s * nt + i, 0)),
                   pl.BlockSpec((1, 16, GROUP_W), lambda s, i: (s, 0, 0))],
        out_shape=[jax.ShapeDtypeStruct((n_seq * seq_len, GROUP_W), F32),
                   jax.ShapeDtypeStruct((n_seq, 16, GROUP_W), F32)],
        scratch_shapes=[pltpu.VMEM((16, GROUP_W), F32)],
        compiler_params=_cparams("parallel", "arbitrary"),
        name="pool_prompt",
    )(z_a, scale.reshape(1, GROUP_W), *w_bd)


def _conv_kernel(u_ref, cw_ref, cb_ref, lw_ref, lb_ref, pb_ref, *rest, tl, nw):
    w_refs, (o_ref, st_ref, ext_ref) = rest[:nw], rest[nw:]
    i = pl.program_id(1)

    @pl.when(i == 0)
    def _():
        ext_ref[0:32, :] = jnp.zeros((32, CONV_C), F32)

    u = u_ref[...]
    g = u[:, :CONV_C] * jax.nn.sigmoid(u[:, CONV_C:])
    ext_ref[32:, :] = g
    y = jnp.zeros((tl, CONV_C), F32)
    for j in range(CONV_W):
        y = y + ext_ref[2 + j:2 + j + tl, :] * cw_ref[j:j + 1, :]
    y = y + cb_ref[...]
    mu = jnp.mean(y, axis=-1, keepdims=True)
    yc = y - mu
    var = jnp.mean(yc * yc, axis=-1, keepdims=True)
    y = yc * lax.rsqrt(var + EPS) * lw_ref[...] + lb_ref[...]
    o_ref[...] = _mm(_act(jax.nn.silu(y), nw == 2), _wts(w_refs)) + pb_ref[...]
    tail = ext_ref[tl:tl + 32, :]
    st_ref[0] = tail
    ext_ref[0:32, :] = tail


def conv_prompt(z_d, cw, cb, lw, lb, pw, pb, n_seq, seq_len, tl=512):
    nt = seq_len // tl
    vec = lambda: pl.BlockSpec((1, CONV_C), lambda s, i: (0, 0))
    return pl.pallas_call(
        functools.partial(_conv_kernel, tl=tl, nw=len(pw)),
        grid=(n_seq, nt),
        in_specs=[pl.BlockSpec((tl, 2 * CONV_C), lambda s, i: (s * nt + i, 0)),
                  pl.BlockSpec((32, CONV_C), lambda s, i: (0, 0)),
                  vec(), vec(), vec(), vec()] + [_full(a, 2) for a in pw],
        out_specs=[pl.BlockSpec((tl, CONV_C), lambda s, i: (s * nt + i, 0)),
                   pl.BlockSpec((1, 32, CONV_C), lambda s, i: (s, 0, 0))],
        out_shape=[jax.ShapeDtypeStruct((n_seq * seq_len, CONV_C), F32),
                   jax.ShapeDtypeStruct((n_seq, 32, CONV_C), F32)],
        scratch_shapes=[pltpu.VMEM((tl + 32, CONV_C), F32)],
        compiler_params=_cparams("parallel", "arbitrary"),
        name="conv_prompt",
    )(z_d, cw, cb.reshape(1, -1), lw.reshape(1, -1), lb.reshape(1, -1), pb.reshape(1, -1), *pw)


def _split_pair(x):
    low = _lane(x.shape) < 64
    return jnp.where(low, x, 0.0), jnp.where(low, pltpu.roll(x, 64, 1), 0.0)


def _hgrn_chunk(zc, lb, onw, tri, s_ref, hp):
    c = zc.shape[0]
    nk = HG_HEADS * HG_DK
    q = zc[:, 0:nk]
    zf = zc[:, nk:2 * nk]
    v_all = zc[:, 2 * nk:2 * nk + 256]
    g_all = zc[:, 2 * nk + 256:2 * nk + 512]
    log_f = jnp.log(lb + (1.0 - lb) * jax.nn.sigmoid(zf))
    k = (1.0 - lb) * jax.nn.sigmoid(-zf)
    hi = log_f.astype(BF16)
    r1 = log_f - hi.astype(F32)
    mid = r1.astype(BF16)
    lo = (r1 - mid.astype(F32)).astype(BF16)
    b = _dot(tri, hi) + _dot(tri, mid) + _dot(tri, lo)
    b_last = b[c - 1:c, :]
    q_inter = q * jnp.exp(b)
    k_tail = k * jnp.exp(b_last - b)
    decay_last = jnp.exp(b_last)
    n_sub = c // HG_SUB
    rowc = _row((HG_SUB, LANES))
    outs = []
    for p in range(2):
        v_pair = _split_pair(v_all[:, LANES * p:LANES * (p + 1)])
        g_pair = _split_pair(g_all[:, LANES * p:LANES * (p + 1)])
        o_pair = []
        for hh in range(2):
            h = 2 * p + hh
            sl = slice(HG_DK * h, HG_DK * (h + 1))
            vh = v_pair[hh]
            vh_t = _act(vh, hp)
            s_old = s_ref[h]
            o = _mm(_act(q_inter[:, sl], hp), _act(s_old, hp))
            bh, qh, kh = b[:, sl], q[:, sl], k[:, sl]
            parts = []
            for ib in range(n_sub):
                r0 = ib * HG_SUB
                b_i, q_i = bh[r0:r0 + HG_SUB], qh[r0:r0 + HG_SUB]
                acc = o[r0:r0 + HG_SUB]
                if ib > 0:
                    ref_b = bh[r0 - 1:r0, :]
                    qt = q_i * jnp.exp(b_i - ref_b)
                    kt = kh[0:r0] * jnp.exp(ref_b - bh[0:r0])
                    a = _mm(_act(qt, hp), _act(kt, hp), nt=True)
                    acc = acc + _mm(_act(a, hp), tuple(t[0:r0] for t in vh_t))
                for s in range(HG_SUB):
                    r = r0 + s
                    e = jnp.exp(jnp.minimum(b_i - bh[r:r + 1, :], 0.0))
                    pr = jnp.where(rowc >= s, q_i * kh[r:r + 1, :] * e, 0.0)
                    acc = acc + jnp.sum(pr, axis=-1, keepdims=True) * vh[r:r + 1, :]
                parts.append(acc)
            o = jnp.concatenate(parts, axis=0) if n_sub > 1 else parts[0]
            dcol = jnp.broadcast_to(decay_last[:, sl], (LANES, LANES)).T
            s_ref[h] = dcol * s_old + _mm(_act(k_tail[:, sl].T, hp), vh_t)
            o = _rms(o, onw, HG_DV) * jax.nn.silu(g_pair[hh])
            o_pair.append(o)
        outs.append(o_pair[0] + pltpu.roll(o_pair[1], 64, 1))
    return jnp.concatenate(outs, axis=1)


def _hgrn_kernel(z_ref, lb_ref, onw_ref, tri_ref, o_ref, st_ref, s_ref, *, tl, hp):
    i = pl.program_id(1)

    @pl.when(i == 0)
    def _():
        s_ref[...] = jnp.zeros_like(s_ref)

    def body(cidx, carry):
        r0 = pl.multiple_of(cidx * HG_CHUNK, HG_CHUNK)
        zc = z_ref[pl.ds(r0, HG_CHUNK), :]
        o_ref[pl.ds(r0, HG_CHUNK), :] = _hgrn_chunk(zc, lb_ref[...], onw_ref[...], tri_ref[...], s_ref, hp)
        return carry

    lax.fori_loop(0, tl // HG_CHUNK, body, 0)
    st_ref[0] = s_ref[...]


def hgrn_prompt(z_c, lb, onw_slot, n_seq, seq_len, hp, tl=512):
    nt = seq_len // tl
    tri = jnp.tril(jnp.ones((HG_CHUNK, HG_CHUNK), F32)).astype(BF16)
    zw = z_c.shape[1]
    return pl.pallas_call(
        functools.partial(_hgrn_kernel, tl=tl, hp=hp),
        grid=(n_seq, nt),
        in_specs=[pl.BlockSpec((tl, zw), lambda s, i: (s * nt + i, 0)),
                  pl.BlockSpec((1, HG_HEADS * HG_DK), lambda s, i: (0, 0)),
                  pl.BlockSpec((1, LANES), lambda s, i: (0, 0)),
                  pl.BlockSpec((HG_CHUNK, HG_CHUNK), lambda s, i: (0, 0))],
        out_specs=[pl.BlockSpec((tl, GROUP_W), lambda s, i: (s * nt + i, 0)),
                   pl.BlockSpec((1, HG_HEADS, HG_DK, LANES), lambda s, i: (s, 0, 0, 0))],
        out_shape=[jax.ShapeDtypeStruct((n_seq * seq_len, GROUP_W), F32),
                   jax.ShapeDtypeStruct((n_seq, HG_HEADS, HG_DK, LANES), F32)],
        scratch_shapes=[pltpu.VMEM((HG_HEADS, HG_DK, LANES), F32)],
        compiler_params=_cparams("parallel", "arbitrary"),
        name="hgrn_prompt",
    )(z_c, lb.reshape(1, -1), onw_slot.reshape(1, LANES), tri)


def _mla_proj_kernel(z_ref, cs_ref, sn_ref, qnw_ref, qv_ref, qvs_ref, kvnw_ref, kv_ref, krv_ref, krvs_ref,
                     *rest, absorb, nw):
    wq_refs, wkv_refs, rest = rest[:nw], rest[nw:2 * nw], rest[2 * nw:]
    if absorb:
        wabs_ref, q_ref, k_ref, v_ref, ckv_ref, kr_ref, qabs_ref = rest
    else:
        q_ref, k_ref, v_ref, ckv_ref, kr_ref = rest
    hp = nw == 2
    z = z_ref[...]
    cos, sin = cs_ref[...], sn_ref[...]
    tm = z.shape[0]
    lane = _lane((tm, LANES))
    is_rope = lane < MLA_ROPE
    kr_raw = z[:, 384:512]
    kr_sw = z[:, 512:640]
    r_kr = lax.rsqrt(jnp.sum(kr_raw * kr_raw, axis=-1, keepdims=True) * (1.0 / MLA_ROPE) + EPS)
    k_rope = (kr_raw * r_kr * krv_ref[...]) * cos + (kr_sw * r_kr * krvs_ref[...]) * sin
    kr_ref[...] = k_rope
    c_kv = _rms(z[:, 256:384], kvnw_ref[...])
    ckv_ref[...] = c_kv
    kv = _mm(_act(c_kv, hp), _wts(wkv_refs))
    v_ref[...] = kv[:, 512:].astype(v_ref.dtype)
    ks = []
    for h in range(MLA_HEADS):
        kraw = kv[:, LANES * h:LANES * (h + 1)]
        ks.append(_rms(kraw, kv_ref[...], MLA_NOPE) + k_rope)
    k_ref[...] = jnp.concatenate(ks, axis=1).astype(k_ref.dtype)
    c_q = _rms(z[:, 0:256], qnw_ref[...])
    qq = _mm(_act(c_q, hp), _wts(wq_refs))
    qs = []
    for h in range(MLA_HEADS):
        x = qq[:, LANES * h:LANES * (h + 1)]
        xs = qq[:, 512 + LANES * h:512 + LANES * (h + 1)]
        x2 = x * x
        r_rope = lax.rsqrt(jnp.sum(jnp.where(is_rope, x2, 0.0), axis=-1, keepdims=True) * (1.0 / MLA_ROPE) + EPS)
        r_nope = lax.rsqrt(jnp.sum(jnp.where(is_rope, 0.0, x2), axis=-1, keepdims=True) * (1.0 / MLA_NOPE) + EPS)
        y = x * jnp.where(is_rope, r_rope, r_nope) * qv_ref[...]
        ysw = xs * r_rope * qvs_ref[...]
        qs.append((y * cos + ysw * sin) * MLA_SCALE)
    q = jnp.concatenate(qs, axis=1)
    q_ref[...] = q.astype(q_ref.dtype)
    if absorb:
        kvec4 = jnp.concatenate([kv_ref[...]] * MLA_HEADS, axis=1)
        qabs_ref[...] = _dot((q * kvec4).astype(BF16), wabs_ref[...])


def mla_proj(z_b, cos_t, sin_t, pw, row0, n_rows, absorb, hp, tile=TOK_TILE):
    assert row0 % tile == 0 and n_rows % tile == 0
    t0 = row0 // tile
    rows = lambda w: pl.BlockSpec((tile, w), lambda i: (i + t0, 0))
    out_rows = lambda w: pl.BlockSpec((tile, w), lambda i: (i, 0))
    consts = [pw["q_norm_w"], pw["qvec"], pw["qvec_sw"], pw["kv_norm_w"], pw["kvec"], pw["krvec"], pw["krvec_sw"]]
    consts += pw["wq"] + pw["wkv"]
    if absorb:
        consts.append(pw["wabs"])
    qkv_dt = F32 if hp else BF16
    out_shape = [jax.ShapeDtypeStruct((n_rows, 512), qkv_dt)] * 3 + [jax.ShapeDtypeStruct((n_rows, LANES), F32)] * 2
    out_specs = [out_rows(512)] * 3 + [out_rows(LANES)] * 2
    if absorb:
        out_shape.append(jax.ShapeDtypeStruct((n_rows, 512), F32))
        out_specs.append(out_rows(512))
    return pl.pallas_call(
        functools.partial(_mla_proj_kernel, absorb=absorb, nw=len(pw["wq"])),
        grid=(n_rows // tile,),
        in_specs=[rows(640), rows(LANES), rows(LANES)] + [_full(a, 1) for a in consts],
        out_specs=out_specs,
        out_shape=out_shape,
        compiler_params=_cparams("parallel"),
        name="mla_proj_s" if absorb else "mla_proj_p",
    )(z_b, cos_t, sin_t, *consts)


def _flash_kernel(qi_ref, kj_ref, q_ref, k_ref, v_ref, o_ref, m_ref, l_ref, acc_ref, *, tq, hp):
    n = pl.program_id(1)
    qi, kj = qi_ref[n], kj_ref[n]

    @pl.when(kj == 0)
    def _():
        m_ref[...] = jnp.full_like(m_ref, -jnp.inf)
        l_ref[...] = jnp.zeros_like(l_ref)
        acc_ref[...] = jnp.zeros_like(acc_ref)

    def operand(ref, sl):
        x = ref[:, sl]
        return _split(x) if hp else (x,)

    def step(masked):
        for h in range(MLA_HEADS):
            sl = slice(LANES * h, LANES * (h + 1))
            s = _mm(operand(q_ref, sl), operand(k_ref, sl), nt=True)
            if masked:
                s = jnp.where(_row((tq, tq)) >= _lane((tq, tq)), s, -jnp.inf)
            m_prev = m_ref[h]
            m_new = jnp.maximum(m_prev, jnp.max(s, axis=-1, keepdims=True))
            alpha = jnp.exp(m_prev - m_new)
            p = jnp.exp(s - jnp.tile(m_new, (1, tq // LANES)))
            l_ref[h] = alpha * l_ref[h] + jnp.sum(p, axis=-1, keepdims=True)
            acc_ref[h] = alpha * acc_ref[h] + _mm(_act(p, hp), operand(v_ref, sl))
            m_ref[h] = m_new

    @pl.when(kj < qi)
    def _():
        step(False)

    @pl.when(kj == qi)
    def _():
        step(True)
        o_ref[...] = jnp.concatenate([acc_ref[h] / l_ref[h] for h in range(MLA_HEADS)], axis=1)


def flash_prompt(q, k, v, n_seq, seq_len, hp, tq=512):
    nq = seq_len // tq
    qi = np.array([i for i in range(nq) for _ in range(i + 1)], np.int32)
    kj = np.array([j for i in range(nq) for j in range(i + 1)], np.int32)
    grid_spec = pltpu.PrefetchScalarGridSpec(
        num_scalar_prefetch=2,
        grid=(n_seq, len(qi)),
        in_specs=[pl.BlockSpec((tq, 512), lambda b, n, qi, kj: (b * nq + qi[n], 0)),
                  pl.BlockSpec((tq, 512), lambda b, n, qi, kj: (b * nq + kj[n], 0)),
                  pl.BlockSpec((tq, 512), lambda b, n, qi, kj: (b * nq + kj[n], 0))],
        out_specs=pl.BlockSpec((tq, 512), lambda b, n, qi, kj: (b * nq + qi[n], 0)),
        scratch_shapes=[pltpu.VMEM((MLA_HEADS, tq, LANES), F32), pltpu.VMEM((MLA_HEADS, tq, LANES), F32),
                        pltpu.VMEM((MLA_HEADS, tq, LANES), F32)])
    return pl.pallas_call(
        functools.partial(_flash_kernel, tq=tq, hp=hp),
        grid_spec=grid_spec,
        out_shape=jax.ShapeDtypeStruct((n_seq * seq_len, 512), F32),
        compiler_params=_cparams("parallel", "arbitrary"),
        name="mla_flash_prompt",
    )(jnp.asarray(qi), jnp.asarray(kj), q, k, v)


def _mla_sample_kernel(pt_ref, cache_c, cache_r, qa_ref, qr_ref, cn_ref, rn_ref, wukt_ref, o_ref,
                       m_ref, l_ref, acc_ref, cbuf, rbuf, csem, rsem, *, npg, n_steps, layer):
    g = pl.program_id(1)
    n = pl.program_id(0) * n_steps + g
    n_total = pl.num_programs(0) * n_steps
    slot = n % 2

    def page_copies(step, s, p):
        page = pt_ref[step * npg + p]
        return (pltpu.make_async_copy(cache_c.at[layer, page], cbuf.at[s, p], csem.at[s]),
                pltpu.make_async_copy(cache_r.at[layer, page], rbuf.at[s, p], rsem.at[s]))

    def start_pages(step, s):
        for p in range(npg):
            for cp in page_copies(step, s, p):
                cp.start()

    @pl.when(n == 0)
    def _():
        start_pages(0, 0)

    @pl.when(n + 1 < n_total)
    def _():
        start_pages(n + 1, 1 - slot)

    for p in range(npg):
        for cp in page_copies(n, slot, p):
            cp.wait()

    @pl.when(g == 0)
    def _():
        m_ref[...] = jnp.full_like(m_ref, -jnp.inf)
        l_ref[...] = jnp.zeros_like(l_ref)
        acc_ref[...] = jnp.zeros_like(acc_ref)

    qa = qa_ref[...].astype(BF16)
    qr = qr_ref[...][:, :MLA_ROPE].astype(BF16)
    nk = MLA_HEADS * MLA_NOPE
    lhs = jnp.concatenate([wukt_ref[...], qa], axis=0)

    def nope_scores(cb):
        big = _dot_nt(lhs, cb)
        kraw_t = big[:nk]
        sq = kraw_t * kraw_t
        rk = [lax.rsqrt(jnp.sum(sq[MLA_NOPE * h:MLA_NOPE * (h + 1)], axis=0, keepdims=True) * (1.0 / MLA_NOPE) + EPS)
              for h in range(MLA_HEADS)]
        rk_rows = jnp.concatenate(rk * (qa.shape[0] // MLA_HEADS), axis=0)
        return big[nk:] * rk_rows

    def update(s_all, cb):
        m_prev = m_ref[...]
        m_new = jnp.maximum(m_prev, jnp.max(s_all, axis=-1, keepdims=True))
        alpha = jnp.exp(m_prev - m_new)
        p_all = jnp.exp(s_all - m_new)
        l_ref[...] = alpha * l_ref[...] + jnp.sum(p_all, axis=-1, keepdims=True)
        acc_ref[...] = alpha * acc_ref[...] + _dot(p_all.astype(BF16), cb)
        m_ref[...] = m_new

    cb_all = jnp.concatenate([cbuf[slot, p].astype(BF16) for p in range(npg)], axis=0)
    kr_all = jnp.concatenate([rbuf[slot, p].astype(BF16) for p in range(npg)], axis=1)
    update(nope_scores(cb_all) + _dot(qr, kr_all), cb_all)

    @pl.when(g == n_steps - 1)
    def _():
        cb = cn_ref[...].astype(BF16)
        s_new = nope_scores(cb) + _dot_nt(qr, rn_ref[...][:, :MLA_ROPE].astype(BF16))
        qtok = _row(s_new.shape) // MLA_HEADS
        s_new = jnp.where(_lane(s_new.shape) <= qtok, s_new, -jnp.inf)
        update(s_new, cb)
        o_ref[...] = acc_ref[...] / l_ref[...]


def mla_sample(page_table, cache_c, cache_rt, layer, qa, qr, c_new, r_new, wukt):
    nb, n_pages = page_table.shape
    npg = PAGES_PER_STEP
    n_steps = n_pages // npg
    rows = qa.shape[1]
    s_pad = c_new.shape[1]

    per_seq = lambda r, w: pl.BlockSpec((None, r, w), lambda b, g, pt: (b, 0, 0))
    full = lambda a: pl.BlockSpec(a.shape, lambda b, g, pt: (0,) * a.ndim)
    any_spec = pl.BlockSpec(memory_space=pl.ANY)
    grid_spec = pltpu.PrefetchScalarGridSpec(
        num_scalar_prefetch=1,
        grid=(nb, n_steps),
        in_specs=[any_spec, any_spec,
                  per_seq(rows, LANES), per_seq(rows, LANES), per_seq(s_pad, LANES), per_seq(s_pad, LANES),
                  full(wukt)],
        out_specs=per_seq(rows, LANES),
        scratch_shapes=[pltpu.VMEM((rows, 1), F32), pltpu.VMEM((rows, 1), F32), pltpu.VMEM((rows, LANES), F32),
                        pltpu.VMEM((2, npg, PAGE_SIZE, MLA_KV_LORA), F32),
                        pltpu.VMEM((2, npg, MLA_ROPE, PAGE_SIZE), F32),
                        pltpu.SemaphoreType.DMA((2,)), pltpu.SemaphoreType.DMA((2,))])
    return pl.pallas_call(
        functools.partial(_mla_sample_kernel, npg=npg, n_steps=n_steps, layer=layer),
        grid_spec=grid_spec,
        out_shape=jax.ShapeDtypeStruct((nb, rows, LANES), F32),
        compiler_params=_cparams("arbitrary", "arbitrary"),
        name="mla_sample",
    )(page_table.reshape(-1), cache_c, cache_rt, qa, qr, c_new, r_new, wukt)


def _matmul_kernel(a_ref, w_ref, o_ref):
    o_ref[...] = _dot(a_ref[...].astype(BF16), w_ref[...])


def matmul(a, w, tile=TOK_TILE, name="matmul"):
    t, kdim = a.shape
    n = w.shape[1]
    return pl.pallas_call(
        _matmul_kernel,
        grid=(t // tile,),
        in_specs=[pl.BlockSpec((tile, kdim), lambda i: (i, 0)), pl.BlockSpec((kdim, n), lambda i: (0, 0))],
        out_specs=pl.BlockSpec((tile, n), lambda i: (i, 0)),
        out_shape=jax.ShapeDtypeStruct((t, n), F32),
        compiler_params=_cparams("parallel"),
        name=name,
    )(a, w)


def _mix_out_kernel(*refs, nw, n_prompt_tiles):
    p_refs, s_refs, x_ref, g_refs, n2_ref = refs[0:4], refs[4:8], refs[8], refs[9:13], refs[13]
    rest = refs[14:]
    w_groups = [rest[nw * g:nw * (g + 1)] for g in range(5)]
    x1_ref, q_ref = rest[5 * nw:]
    hp = nw == 2
    is_prompt = pl.program_id(0) < n_prompt_tiles
    acc = x_ref[...]
    for p_ref, s_ref, g_ref, w_refs in zip(p_refs, s_refs, g_refs, w_groups[:4]):
        o = jnp.where(is_prompt, p_ref[...], s_ref[...])
        acc = acc + _mm(_act(_rms(o, g_ref[...], GROUP_W), hp), _wts(w_refs))
    x1_ref[...] = acc
    q_ref[...] = _mm(_act(_rms(acc, n2_ref[...]), hp), _wts(w_groups[4]))


def mix_out(o_prompt, o_sample, x, gw, w_out, n2w, wq, tile=TOK_TILE):
    t = x.shape[0]
    npt = o_prompt[0].shape[0] // tile
    assert all(a.shape[0] == tile for a in o_sample) and t == (npt + 1) * tile
    prompt_rows = lambda a: pl.BlockSpec((tile, a.shape[1]), lambda i: (jnp.minimum(i, npt - 1), 0))
    acts = list(o_prompt) + list(o_sample) + [x]
    consts = list(gw) + [n2w.reshape(1, -1)] + [a for w in w_out for a in w] + list(wq)
    return pl.pallas_call(
        functools.partial(_mix_out_kernel, nw=len(wq), n_prompt_tiles=npt),
        grid=(t // tile,),
        in_specs=[prompt_rows(a) for a in o_prompt] + [_full(a, 1) for a in o_sample]
                 + [pl.BlockSpec((tile, D_MODEL), lambda i: (i, 0))] + [_full(a, 1) for a in consts],
        out_specs=[pl.BlockSpec((tile, D_MODEL), lambda i: (i, 0)), pl.BlockSpec((tile, CA_W), lambda i: (i, 0))],
        out_shape=[jax.ShapeDtypeStruct((t, D_MODEL), F32), jax.ShapeDtypeStruct((t, CA_W), F32)],
        compiler_params=_cparams("parallel"),
        name="mix_out",
    )(*acts, *consts)


def _cross_kernel(q_ref, mk_ref, mv_ref, x_ref, qn_ref, *rest, nw):
    wo_refs, (o_ref,) = rest[:nw], rest[nw:]
    hp = nw == 2
    q = q_ref[...]
    outs = []
    for h in range(CA_HEADS):
        sl = slice(CA_HD * h, CA_HD * (h + 1))
        qh = _rms(q[:, sl], qn_ref[...])
        s = _mm(_act(qh, hp), _act(mk_ref[:, sl], hp), nt=True) * (CA_HD ** -0.5)
        s = s - jnp.max(s, axis=-1, keepdims=True)
        e = jnp.exp(s)
        p = e / jnp.sum(e, axis=-1, keepdims=True)
        outs.append(_mm(_act(p, hp), _act(mv_ref[:, sl], hp)))
    o = jnp.concatenate(outs, axis=1)
    o_ref[...] = x_ref[...] + _mm(_act(o, hp), _wts(wo_refs))


def cross_attend(q, mem_k, mem_v, x, qnw, wo, n_seq, seq_len, tl):
    nt = seq_len // tl
    mem_spec = pl.BlockSpec((None, N_MEM, CA_W), lambda s, i: (s, 0, 0))
    return pl.pallas_call(
        functools.partial(_cross_kernel, nw=len(wo)),
        grid=(n_seq, nt),
        in_specs=[pl.BlockSpec((tl, CA_W), lambda s, i: (s * nt + i, 0)),
                  mem_spec, mem_spec,
                  pl.BlockSpec((tl, D_MODEL), lambda s, i: (s * nt + i, 0)),
                  pl.BlockSpec((1, CA_HD), lambda s, i: (0, 0))] + [_full(a, 2) for a in wo],
        out_specs=pl.BlockSpec((tl, D_MODEL), lambda s, i: (s * nt + i, 0)),
        out_shape=jax.ShapeDtypeStruct((n_seq * seq_len, D_MODEL), F32),
        compiler_params=_cparams("parallel", "parallel"),
        name="cross_attend",
    )(q, mem_k, mem_v, x, qnw.reshape(1, CA_HD), *wo)


SAMPLE_SEQS_PER_STEP = 8


def _cross_sample_kernel(q_ref, mk_ref, mv_ref, x_ref, qn_ref, *rest, nw, s_len):
    wo_refs, (o_ref,) = rest[:nw], rest[nw:]
    hp = nw == 2
    q = q_ref[...]
    rows = CA_HEADS * s_len
    shape = (rows, N_MEM * CA_HEADS)
    own_head = (_lane(shape) % CA_HEADS) == (_row(shape) // s_len)
    outs = []
    for s in range(SAMPLE_SEQS_PER_STEP):
        qs = q[s * s_len:(s + 1) * s_len]
        qh = jnp.concatenate([qs[:, CA_HD * h:CA_HD * (h + 1)] for h in range(CA_HEADS)], axis=0)
        qn = _rms(qh, qn_ref[...])
        sc = _mm(_act(qn, hp), _act(mk_ref[s], hp), nt=True) * (CA_HD ** -0.5)
        sc = jnp.where(own_head, sc, -jnp.inf)
        sc = sc - jnp.max(sc, axis=-1, keepdims=True)
        e = jnp.exp(sc)
        p = e / jnp.sum(e, axis=-1, keepdims=True)
        o = _mm(_act(p, hp), _act(mv_ref[s], hp))
        outs.append(jnp.concatenate([o[h * s_len:(h + 1) * s_len] for h in range(CA_HEADS)], axis=1))
    o_all = jnp.concatenate(outs, axis=0)
    o_ref[...] = x_ref[...] + _mm(_act(o_all, hp), _wts(wo_refs))


def cross_attend_sample(q, mem_k, mem_v, x, qnw, wo, s_len, layer):
    sb = SAMPLE_SEQS_PER_STEP
    n_seq = q.shape[0] // s_len
    rows = sb * s_len
    mem_spec = pl.BlockSpec((None, sb, N_MEM * CA_HEADS, CA_HD), lambda i: (layer, i, 0, 0))
    return pl.pallas_call(
        functools.partial(_cross_sample_kernel, nw=len(wo), s_len=s_len),
        grid=(n_seq // sb,),
        in_specs=[pl.BlockSpec((rows, CA_W), lambda i: (i, 0)), mem_spec, mem_spec,
                  pl.BlockSpec((rows, D_MODEL), lambda i: (i, 0)),
                  pl.BlockSpec((1, CA_HD), lambda i: (0, 0))] + [_full(a, 1) for a in wo],
        out_specs=pl.BlockSpec((rows, D_MODEL), lambda i: (i, 0)),
        out_shape=jax.ShapeDtypeStruct((n_seq * s_len, D_MODEL), F32),
        compiler_params=_cparams("parallel"),
        name="cross_attend_sample",
    )(q, mem_k, mem_v, x, qnw.reshape(1, CA_HD), *wo)


def _swiglu_into(h, wg_refs, wu_refs, wd_refs, o_ref, width):
    hp = len(wg_refs) == 2
    for c0 in range(0, width, FF_CHUNK):
        cols = (slice(None), slice(c0, c0 + FF_CHUNK))
        g = _mm(h, _wts(wg_refs, cols))
        u = _mm(h, _wts(wu_refs, cols))
        a = _act(jax.nn.silu(g) * u, hp)
        o_ref[...] += _mm(a, _wts(wd_refs, (slice(c0, c0 + FF_CHUNK), slice(None))))


def _ffn_kernel(x_ref, nw_ref, *rest, nw, width):
    wg, wu, wd, (o_ref,) = rest[:nw], rest[nw:2 * nw], rest[2 * nw:3 * nw], rest[3 * nw:]
    x = x_ref[...]

    @pl.when(pl.program_id(1) == 0)
    def _():
        o_ref[...] = x

    _swiglu_into(_act(_rms(x, nw_ref[...]), nw == 2), wg, wu, wd, o_ref, width)


def ffn_dense(x, nw, wg, wu, wd, ff_block, tile=TOK_TILE):
    t = x.shape[0]
    up = lambda: pl.BlockSpec((D_MODEL, ff_block), lambda i, j: (0, j))
    down = lambda: pl.BlockSpec((ff_block, D_MODEL), lambda i, j: (j, 0))
    n = len(wg)
    return pl.pallas_call(
        functools.partial(_ffn_kernel, nw=n, width=ff_block),
        grid=(t // tile, D_FF // ff_block),
        in_specs=[pl.BlockSpec((tile, D_MODEL), lambda i, j: (i, 0)), pl.BlockSpec((1, D_MODEL), lambda i, j: (0, 0))]
                 + [up() for _ in range(2 * n)] + [down() for _ in range(n)],
        out_specs=pl.BlockSpec((tile, D_MODEL), lambda i, j: (i, 0)),
        out_shape=jax.ShapeDtypeStruct((t, D_MODEL), F32),
        compiler_params=_cparams("parallel", "arbitrary"),
        name="ffn_dense",
    )(x, nw.reshape(1, -1), *wg, *wu, *wd)


def _router_kernel(x_ref, nw_ref, r_ref, h_ref, g_ref):
    h = _rms(x_ref[...], nw_ref[...])
    h_ref[...] = h
    logits = jnp.dot(h, r_ref[...], preferred_element_type=F32, precision=lax.Precision.HIGHEST)
    lane = _lane(logits.shape)
    neg = -jnp.inf
    l1 = jnp.where(lane < N_EXPERTS, logits, neg)
    m1 = jnp.max(l1, axis=-1, keepdims=True)
    i1 = jnp.min(jnp.where(l1 == m1, lane, LANES), axis=-1, keepdims=True)
    l2 = jnp.where(lane == i1, neg, l1)
    m2 = jnp.max(l2, axis=-1, keepdims=True)
    i2 = jnp.min(jnp.where(l2 == m2, lane, LANES), axis=-1, keepdims=True)
    e2 = jnp.exp(m2 - m1)
    den = 1.0 + e2
    gates = jnp.where(lane == i1, 1.0 / den, jnp.where(lane == i2, e2 / den, 0.0))
    g_ref[...] = jnp.where(lane == N_EXPERTS, i1.astype(F32), jnp.where(lane == N_EXPERTS + 1, i2.astype(F32), gates))


def moe_router(x, nw, router_pad, tile=TOK_TILE):
    t = x.shape[0]
    return pl.pallas_call(
        _router_kernel,
        grid=(t // tile,),
        in_specs=[pl.BlockSpec((tile, D_MODEL), lambda i: (i, 0)), pl.BlockSpec((1, D_MODEL), lambda i: (0, 0)),
                  pl.BlockSpec((D_MODEL, LANES), lambda i: (0, 0))],
        out_specs=[pl.BlockSpec((tile, D_MODEL), lambda i: (i, 0)), pl.BlockSpec((tile, LANES), lambda i: (i, 0))],
        out_shape=[jax.ShapeDtypeStruct((t, D_MODEL), F32), jax.ShapeDtypeStruct((t, LANES), F32)],
        compiler_params=_cparams("parallel"),
        name="moe_router",
    )(x, nw.reshape(1, -1), router_pad)


def _experts_kernel(be_ref, nv_ref, src_ref, nxt_ref, h_hbm, wg_ref, wu_ref, wd_ref, o_ref, xbuf, sem, *, tile):
    b = pl.program_id(0)
    slot = b % 2
    nv = nv_ref[0]

    def row_copy(idx_ref, i, s):
        return pltpu.make_async_copy(h_hbm.at[pl.ds(idx_ref[0, i], 1), :], xbuf.at[s, pl.ds(i, 1), :], sem.at[s])

    def start_rows(idx_ref, s):
        def body(i, c):
            row_copy(idx_ref, i, s).start()
            return c
        lax.fori_loop(0, tile, body, 0, unroll=8)

    @pl.when((b == 0) & (nv > 0))
    def _():
        start_rows(src_ref, 0)

    @pl.when(b + 1 < nv)
    def _():
        start_rows(nxt_ref, 1 - slot)

    o_ref[...] = jnp.zeros_like(o_ref)

    @pl.when(b < nv)
    def _():
        def body(i, c):
            row_copy(src_ref, i, slot).wait()
            return c
        lax.fori_loop(0, tile, body, 0, unroll=8)
        _swiglu_into((xbuf[slot].astype(BF16),), (wg_ref,), (wu_ref,), (wd_ref,), o_ref, D_FF)


def moe_experts(h, src, blk_e, n_valid, wg, wu, wd, j, tile=MOE_TILE):
    n_blk = src.shape[0]
    wspec = lambda a: pl.BlockSpec((None, None) + a.shape[2:], lambda b, be, nv: (j, be[b], 0, 0))
    idx_spec = lambda f: pl.BlockSpec((None, 1, tile), f, memory_space=pltpu.SMEM)
    grid_spec = pltpu.PrefetchScalarGridSpec(
        num_scalar_prefetch=2,
        grid=(n_blk,),
        in_specs=[idx_spec(lambda b, be, nv: (b, 0, 0)),
                  idx_spec(lambda b, be, nv: (jnp.minimum(b + 1, n_blk - 1), 0, 0)),
                  pl.BlockSpec(memory_space=pl.ANY), wspec(wg), wspec(wu), wspec(wd)],
        out_specs=pl.BlockSpec((tile, D_MODEL), lambda b, be, nv: (b, 0)),
        scratch_shapes=[pltpu.VMEM((2, tile, D_MODEL), F32), pltpu.SemaphoreType.DMA((2,))])
    return pl.pallas_call(
        functools.partial(_experts_kernel, tile=tile),
        grid_spec=grid_spec,
        out_shape=jax.ShapeDtypeStruct((n_blk * tile, D_MODEL), F32),
        compiler_params=_cparams("arbitrary"),
        name="moe_experts",
    )(blk_e, n_valid, src, src, h, wg, wu, wd)


def moe_ffn(x, nw, router_pad, wg, wu, wd, j):
    t = x.shape[0]
    h, gfull = moe_router(x, nw, router_pad)
    e_idx = gfull[:, N_EXPERTS:N_EXPERTS + 2].astype(jnp.int32)
    g2 = jnp.take_along_axis(gfull[:, :N_EXPERTS], e_idx, axis=1)
    sel = (jnp.arange(N_EXPERTS, dtype=jnp.int32)[None, :] == e_idx[:, 0:1]) | (
        jnp.arange(N_EXPERTS, dtype=jnp.int32)[None, :] == e_idx[:, 1:2])
    pos = jnp.cumsum(sel.astype(jnp.int32), axis=0) - 1
    counts = pos[-1] + 1
    nb_e = (counts + MOE_TILE - 1) // MOE_TILE
    blk_end = jnp.cumsum(nb_e)
    row_start = (blk_end - nb_e) * MOE_TILE
    slot = row_start[None, :] + pos
    assert (2 * t) % MOE_TILE == 0
    n_blk = (2 * t + N_EXPERTS * MOE_TILE) // MOE_TILE
    stride = t + MOE_TILE
    big = N_EXPERTS * stride
    tok = jnp.arange(t, dtype=jnp.int32)[:, None]
    real = e_idx * stride + tok
    pad_e = nb_e * MOE_TILE - counts
    fill = jnp.arange(MOE_TILE, dtype=jnp.int32)[None, :]
    dummy = jnp.where(fill < pad_e[:, None],
                      jnp.arange(N_EXPERTS, dtype=jnp.int32)[:, None] * stride + t + fill, big)
    keys = jnp.sort(jnp.concatenate([real.reshape(-1), dummy.reshape(-1)]))
    src = keys % stride
    src = jnp.where((src >= t) | (keys >= big), 0, src).reshape(n_blk, 1, MOE_TILE)
    blk_e = jnp.minimum(jnp.searchsorted(blk_end, jnp.arange(n_blk, dtype=jnp.int32), side="right"),
                        N_EXPERTS - 1).astype(jnp.int32)
    y_buf = moe_experts(h, src, blk_e, blk_end[-1:].astype(jnp.int32), wg, wu, wd, j)
    s2 = jnp.take_along_axis(slot, e_idx, axis=1)
    y = y_buf[s2[:, 0]] * g2[:, 0:1] + y_buf[s2[:, 1]] * g2[:, 1:2]
    return x + y


def _zeros(r, c):
    return jnp.zeros((r, c), F32)


def _prep_w_in(w, hp):
    d = w.shape[0]
    k_r = w[:, 640:672]
    kr_slot = jnp.concatenate([k_r, _zeros(d, 96)], axis=1)
    kr_sw = jnp.concatenate([k_r[:, 16:], k_r[:, :16], _zeros(d, 96)], axis=1)
    return _w(jnp.concatenate([w[:, 0:256], w[:, 256:640], kr_slot, kr_sw, w[:, 672:2208], w[:, 2208:2720]],
                              axis=1), hp)


W_IN_SPLITS = (256, 640, 1536, 512)


def _slot_vec(rope_w, nope_w):
    z = jnp.zeros((32,), F32)
    return jnp.concatenate([rope_w, nope_w, z]).reshape(1, LANES)


def _prep_mla(lw, hp):
    wq, wkv = lw["mla_w_uq"], lw["mla_w_ukv"]
    slots, sw = [], []
    for h in range(MLA_HEADS):
        nope = wq[:, 96 * h:96 * h + 64]
        rope = wq[:, 96 * h + 64:96 * h + 96]
        slots.append(jnp.concatenate([rope, nope, _zeros(MLA_Q_LORA, 32)], axis=1))
        sw.append(jnp.concatenate([rope[:, 16:], rope[:, :16], _zeros(MLA_Q_LORA, 96)], axis=1))
    kslots = [jnp.concatenate([_zeros(128, 32), wkv[:, 128 * h:128 * h + 64], _zeros(128, 32)], axis=1)
              for h in range(MLA_HEADS)]
    vslots = [jnp.concatenate([wkv[:, 128 * h + 64:128 * h + 128], _zeros(128, 64)], axis=1)
              for h in range(MLA_HEADS)]
    z96 = jnp.zeros((96,), F32)
    qr, kr = lw["mla_qn_rope_w"], lw["mla_kn_rope_w"]
    wabs = jnp.zeros((512, 512), F32)
    for h in range(MLA_HEADS):
        wabs = wabs.at[128 * h + 32:128 * h + 96, 128 * h:128 * h + 128].set(wkv[:, 128 * h:128 * h + 64].T)
    wuv_bd = jnp.zeros((512, 256), F32)
    for h in range(MLA_HEADS):
        wuv_bd = wuv_bd.at[128 * h:128 * h + 128, 64 * h:64 * h + 64].set(wkv[:, 128 * h + 64:128 * h + 128])
    return dict(
        q_norm_w=lw["mla_q_norm_w"].reshape(1, -1),
        wq=_w(jnp.concatenate(slots + sw, axis=1), hp),
        qvec=_slot_vec(qr, lw["mla_qn_nope_w"]),
        qvec_sw=jnp.concatenate([qr[16:], qr[:16], z96]).reshape(1, LANES),
        kv_norm_w=lw["mla_kv_norm_w"].reshape(1, -1),
        wkv=_w(jnp.concatenate(kslots + vslots, axis=1), hp),
        kvec=_slot_vec(jnp.zeros((32,), F32), lw["mla_kn_nope_w"]),
        krvec=jnp.concatenate([kr, z96]).reshape(1, LANES),
        krvec_sw=jnp.concatenate([kr[16:], kr[:16], z96]).reshape(1, LANES),
        wabs=wabs.astype(BF16),
        wukt=jnp.concatenate([wkv[:, 128 * h:128 * h + 64] for h in range(MLA_HEADS)], axis=1).T.astype(BF16),
        wuv_bd=wuv_bd.astype(BF16),
    )


def _rope_tables(pos):
    half = MLA_ROPE // 2
    inv = 1.0 / (ROPE_THETA ** (jnp.arange(half, dtype=F32) / half))
    ang = pos.astype(F32)[:, None] * inv[None, :]
    cos, sin = jnp.cos(ang), jnp.sin(ang)
    n = pos.shape[0]
    cos_t = jnp.concatenate([cos, cos, jnp.ones((n, 64), F32), jnp.zeros((n, 32), F32)], axis=1)
    sin_t = jnp.concatenate([-sin, sin, jnp.zeros((n, 96), F32)], axis=1)
    return cos_t, sin_t


def _pad_new(a):
    n, s, w = a.shape
    return jnp.concatenate([a, jnp.zeros((n, 8 - s, w), a.dtype)], axis=1)


def _pad_rows(w, rows_per, pad_to):
    g = w.shape[0] // rows_per
    w = w.reshape(g, rows_per, -1)
    return jnp.concatenate([w, jnp.zeros((g, pad_to - rows_per, w.shape[-1]), w.dtype)], axis=1).reshape(
        g * pad_to, -1)


def _sample_pool(u, prefix, w_bd, scale):
    n, L, c = u.shape
    ext = jnp.concatenate([prefix, u], axis=1)
    cs = jnp.concatenate([jnp.zeros((n, 1, c), F32), jnp.cumsum(ext, axis=1)], axis=1)
    hi = cs[:, POOL_PAD + 1:]
    pooled = []
    for g, w in enumerate(POOL_WINDOWS):
        sl = slice(64 * g, 64 * (g + 1))
        lo = cs[:, POOL_PAD + 1 - w:POOL_PAD + 1 - w + L, sl]
        pooled.append((hi[..., sl] - lo) / float(w))
    d = jnp.concatenate(pooled, axis=-1) - u
    y = jnp.dot(d.reshape(n * L, c).astype(BF16), w_bd, preferred_element_type=F32) * scale
    return y, ext[:, -POOL_PAD:]


def _sample_hgrn(zc, s0, lb, onw):
    n, L, _ = zc.shape
    q = zc[..., 0:512].reshape(n, L, HG_HEADS, HG_DK)
    zf = zc[..., 512:1024].reshape(n, L, HG_HEADS, HG_DK)
    v = zc[..., 1024:1280].reshape(n, L, HG_HEADS, HG_DV)
    g = zc[..., 1280:1536].reshape(n, L, HG_HEADS, HG_DV)
    lbh = lb.reshape(HG_HEADS, HG_DK)
    f = lbh + (1.0 - lbh) * jax.nn.sigmoid(zf)
    k = (1.0 - lbh) * jax.nn.sigmoid(-zf)
    s = s0
    outs = []
    for t in range(L):
        s = f[:, t][..., None] * s + k[:, t][..., None] * v[:, t][:, :, None, :]
        outs.append(jnp.sum(s * q[:, t][..., None], axis=2))
    o = jnp.stack(outs, axis=1)
    o = _rms(o, onw, HG_DV) * jax.nn.silu(g)
    return o.reshape(n * L, GROUP_W), s


def _sample_conv(u, prefix, cw, cb, lw, lb, pw, pb):
    n, L, _ = u.shape
    g = u[..., :CONV_C] * jax.nn.sigmoid(u[..., CONV_C:])
    ext = jnp.concatenate([prefix, g], axis=1)
    y = sum(ext[:, j:j + L, :] * cw[j][None, None, :] for j in range(CONV_W)) + cb
    mu = jnp.mean(y, axis=-1, keepdims=True)
    yc = y - mu
    var = jnp.mean(yc * yc, axis=-1, keepdims=True)
    y = yc * lax.rsqrt(var + EPS) * lw + lb
    y = jnp.dot(jax.nn.silu(y).reshape(n * L, CONV_C).astype(BF16), pw, preferred_element_type=F32) + pb
    return y, ext[:, -CONV_PAD:]


def kernel(x_prompt, x_sample, cache_kv_latent, cache_k_rope, cache_mem_k, cache_mem_v, state_pool, state_hgrn, state_conv, page_table, mem_prompt, norm1_w, w_in, pool_w, pool_scale, mla_q_norm_w, mla_w_uq, mla_kv_norm_w, mla_w_ukv, mla_qn_nope_w, mla_qn_rope_w, mla_kn_nope_w, mla_kn_rope_w, hg_lb_param, hg_onorm_w, conv_w, conv_b, conv_ln_w, conv_ln_b, conv_pw_w, conv_pw_b, grp_norm_w, w_out, norm2_w, mem_norm_w, ca_wq, ca_wk, ca_wv, ca_qn_w, ca_kn_w, ca_wo, norm3_w, ffn_w_gate, ffn_w_up, ffn_w_down, moe_router, moe_w_gate, moe_w_up, moe_w_down):
    bp, seq, d = x_prompt.shape
    nb, s_len, _ = x_sample.shape
    depth = w_in.shape[0]
    tp, ts = bp * seq, nb * s_len
    n_past = page_table.shape[1] * PAGE_SIZE
    n_mem = mem_prompt.shape[1]
    first_moe = 1

    sm = jax.nn.softmax(hg_lb_param.astype(F32), axis=0)
    lower_bounds = jnp.cumsum(sm, axis=0) - sm[:1]

    pos = jnp.concatenate([jnp.tile(jnp.arange(seq, dtype=jnp.int32), bp),
                           jnp.tile(n_past + jnp.arange(s_len, dtype=jnp.int32), nb)])
    cos_t, sin_t = _rope_tables(pos)
    cache_rt = jnp.swapaxes(cache_k_rope, 2, 3)
    moe_wg, moe_wu, moe_wd = moe_w_gate.astype(BF16), moe_w_up.astype(BF16), moe_w_down.astype(BF16)
    mem_k_all = cache_mem_k.reshape(depth, nb, n_mem * CA_HEADS, CA_HD)
    mem_v_all = cache_mem_v.reshape(depth, nb, n_mem * CA_HEADS, CA_HD)

    x = jnp.concatenate([x_prompt.reshape(tp, d), x_sample.reshape(ts, d)], axis=0)
    mem2d = mem_prompt.reshape(bp * n_mem, d)
    outs = [[] for _ in range(12)]

    for l in range(depth):
        hp = l <= first_moe
        lw = dict(mla_q_norm_w=mla_q_norm_w[l], mla_w_uq=mla_w_uq[l], mla_kv_norm_w=mla_kv_norm_w[l],
                  mla_w_ukv=mla_w_ukv[l], mla_qn_nope_w=mla_qn_nope_w[l], mla_qn_rope_w=mla_qn_rope_w[l],
                  mla_kn_nope_w=mla_kn_nope_w[l], mla_kn_rope_w=mla_kn_rope_w[l])
        pw = _prep_mla(lw, hp)
        z_a, z_b, z_c, z_d = norm_matmul(x, norm1_w[l], _prep_w_in(w_in[l], hp), W_IN_SPLITS, name="norm_w_in")

        w_bd = _w(jax.scipy.linalg.block_diag(*[pool_w[l, g] for g in range(4)]), hp)
        o_a_p, pool_st = pool_prompt(z_a, w_bd, pool_scale[l], bp, seq)
        o_a_s, pool_st_s = _sample_pool(z_a[tp:].reshape(nb, s_len, GROUP_W), state_pool[l], w_bd[0], pool_scale[l])
        outs[0].append(pool_st[:, 1:])
        outs[7].append(pool_st_s)

        q_p, k_p, v_p, ckv_p, kr_p = mla_proj(z_b, cos_t, sin_t, pw, 0, tp, absorb=False, hp=hp)
        o_b_p = flash_prompt(q_p, k_p, v_p, bp, seq, hp)
        q_s, _, _, ckv_s, kr_s, qabs_s = mla_proj(z_b, cos_t, sin_t, pw, tp, ts, absorb=True, hp=hp)
        rows = MLA_HEADS * s_len
        o_lat = mla_sample(page_table, cache_kv_latent, cache_rt, l,
                           qabs_s.reshape(nb, rows, LANES), q_s.astype(F32).reshape(nb, rows, LANES),
                           _pad_new(ckv_s.reshape(nb, s_len, LANES)), _pad_new(kr_s.reshape(nb, s_len, LANES)),
                           pw["wukt"])
        o_b_s = matmul(o_lat.reshape(ts, 512), pw["wuv_bd"], name="mla_v_up")
        o_b_s = jnp.concatenate([o_b_s.reshape(ts, 4, 64), jnp.zeros((ts, 4, 64), F32)], axis=-1).reshape(ts, 512)
        outs[1].append(ckv_p.reshape(bp, seq, MLA_KV_LORA))
        outs[2].append(kr_p[:, :MLA_ROPE].reshape(bp, seq, MLA_ROPE))
        outs[8].append(ckv_s.reshape(nb, s_len, MLA_KV_LORA))
        outs[9].append(kr_s[:, :MLA_ROPE].reshape(nb, s_len, MLA_ROPE))

        onw_slot = jnp.concatenate([hg_onorm_w[l], jnp.zeros((64,), F32)])
        o_c_p, hg_st = hgrn_prompt(z_c, lower_bounds[l], onw_slot, bp, seq, hp)
        o_c_s, hg_st_s = _sample_hgrn(z_c[tp:].reshape(nb, s_len, -1), state_hgrn[l], lower_bounds[l], hg_onorm_w[l])
        outs[3].append(hg_st[..., :HG_DV])
        outs[10].append(hg_st_s)

        cw_pad = jnp.concatenate([conv_w[l], jnp.zeros((1, CONV_C), F32)], axis=0)
        pw_c = _w(conv_pw_w[l], hp)
        o_d_p, conv_st = conv_prompt(z_d, cw_pad, conv_b[l], conv_ln_w[l], conv_ln_b[l], pw_c, conv_pw_b[l], bp, seq)
        o_d_s, conv_st_s = _sample_conv(z_d[tp:].reshape(nb, s_len, -1), state_conv[l], conv_w[l], conv_b[l],
                                        conv_ln_w[l], conv_ln_b[l], pw_c[0], conv_pw_b[l])
        outs[4].append(conv_st[:, 2:])
        outs[11].append(conv_st_s)

        gnw = grp_norm_w[l]
        gw = [gnw[0:256].reshape(1, -1), _pad_rows(gnw[256:512].reshape(-1, 1), 64, 128).reshape(1, -1),
              gnw[512:768].reshape(1, -1), gnw[768:1024].reshape(1, -1)]
        wo_l = w_out[l]
        w_out_parts = [_w(wo_l[0:256], hp), _w(_pad_rows(wo_l[256:512], 64, 128), hp),
                       _w(wo_l[512:768], hp), _w(wo_l[768:1024], hp)]
        x1, q_ca = mix_out((o_a_p, o_b_p, o_c_p, o_d_p), (o_a_s, o_b_s, o_c_s, o_d_s), x, gw, w_out_parts,
                           norm2_w[l], _w(ca_wq[l], hp))

        w_kv = _w(jnp.concatenate([ca_wk[l], ca_wv[l]], axis=1), hp)
        mk_p, mv_p = norm_matmul(mem2d, mem_norm_w[l], w_kv, (CA_W, CA_W), slot_norm_w=ca_kn_w[l],
                                 tile=min(TOK_TILE, mem2d.shape[0]), name="memory_kv")
        outs[5].append(mk_p.reshape(bp, n_mem, CA_HEADS, CA_HD))
        outs[6].append(mv_p.reshape(bp, n_mem, CA_HEADS, CA_HD))

        wo_ca = _w(ca_wo[l], hp)
        x2_p = cross_attend(q_ca, mk_p.reshape(bp, n_mem, CA_W), mv_p.reshape(bp, n_mem, CA_W), x1,
                            ca_qn_w[l], wo_ca, bp, seq, tl=min(512, seq))
        x2_s = cross_attend_sample(q_ca[tp:], mem_k_all, mem_v_all, x1[tp:], ca_qn_w[l], wo_ca, s_len, l)
        x2 = jnp.concatenate([x2_p, x2_s], axis=0)

        j = l // 2
        if l % 2 == 0:
            x = ffn_dense(x2, norm3_w[l], _w(ffn_w_gate[j], hp), _w(ffn_w_up[j], hp), _w(ffn_w_down[j], hp),
                          ff_block=FF_CHUNK if hp else D_FF)
        else:
            router_pad = jnp.concatenate([moe_router[j], jnp.zeros((d, LANES - N_EXPERTS), F32)], axis=1)
            x = moe_ffn(x2, norm3_w[l], router_pad, moe_wg, moe_wu, moe_wd, j)

    st = lambda k: jnp.stack(outs[k])
    return (x[:tp].reshape(bp, seq, d), x[tp:].reshape(nb, s_len, d),
            st(0), st(1), st(2), st(3), st(4), st(5), st(6), st(7), st(8), st(9), st(10), st(11))
```

```python
import functools

import numpy as np
import jax
import jax.numpy as jnp
from jax import lax
from jax.experimental import pallas as pl
from jax.experimental.pallas import tpu as pltpu

F32 = jnp.float32
BF16 = jnp.bfloat16
EPS = 1e-6

D_MODEL = 1024
GROUP_W = 256
POOL_WINDOWS = (2, 4, 8, 16)
POOL_PAD = 15
MLA_HEADS = 4
MLA_NOPE = 64
MLA_ROPE = 32
MLA_V = 64
MLA_Q_LORA = 256
MLA_KV_LORA = 128
MLA_SCALE = (MLA_NOPE + MLA_ROPE) ** -0.5
ROPE_THETA = 10000.0
PAGE_SIZE = 128
HG_HEADS = 4
HG_DK = 128
HG_DV = 64
CONV_W = 31
CONV_PAD = 30
CONV_C = 256
N_MEM = 256
CA_HEADS = 4
CA_HD = 128
CA_W = 512
D_FF = 2816
N_EXPERTS = 8

LANES = 128
VMEM_LIMIT_BYTES = 56 * 1024 * 1024
TOK_TILE = 512
FF_CHUNK = 256
MOE_TILE = 512
HG_CHUNK = 64
HG_SUB = 16
PAGES_PER_STEP = 32


def _cparams(*sem):
    return pltpu.CompilerParams(dimension_semantics=sem, vmem_limit_bytes=VMEM_LIMIT_BYTES)


def _rms(x, w, n=None):
    n = x.shape[-1] if n is None else n
    r = lax.rsqrt(jnp.sum(x * x, axis=-1, keepdims=True) * (1.0 / n) + EPS)
    return x * r * w


def _dot(a, b):
    return jnp.dot(a, b, preferred_element_type=F32)


def _dot_nt(a, b):
    return lax.dot_general(a, b, (((1,), (1,)), ((), ())), preferred_element_type=F32)


def _lane(shape):
    return lax.broadcasted_iota(jnp.int32, shape, len(shape) - 1)


def _row(shape):
    return lax.broadcasted_iota(jnp.int32, shape, len(shape) - 2)


def _split(a):
    hi = a.astype(BF16)
    return hi, (a - hi.astype(F32)).astype(BF16)


def _act(a, hp):
    return _split(a) if hp else (a.astype(BF16),)


def _wts(refs, idx=None):
    return tuple(r[...] if idx is None else r[idx] for r in refs)


def _mm(a, b, nt=False):
    d = _dot_nt if nt else _dot
    out = d(a[0], b[0])
    if len(a) > 1:
        out = out + d(a[1], b[0]) + d(a[0], b[1])
    return out


def _w(w, hp):
    return list(_split(w)) if hp else [w.astype(BF16)]


def _full(a, nargs):
    zeros = (0,) * a.ndim
    if nargs == 1:
        return pl.BlockSpec(a.shape, lambda i: zeros)
    if nargs == 2:
        return pl.BlockSpec(a.shape, lambda i, j: zeros)
    return pl.BlockSpec(a.shape, lambda i, j, k: zeros)


def _norm_matmul_kernel(x_ref, nw_ref, *rest, splits, slot_norm, nw):
    w_refs, rest = rest[:nw], rest[nw:]
    if slot_norm:
        sw_ref, out_refs = rest[0], rest[1:]
    else:
        out_refs = rest
    h = _act(_rms(x_ref[...], nw_ref[...]), nw == 2)
    off = 0
    for idx, (o_ref, n) in enumerate(zip(out_refs, splits)):
        y = _mm(h, _wts(w_refs, (slice(None), slice(off, off + n))))
        if slot_norm and idx == 0:
            y = jnp.concatenate(
                [_rms(y[:, s:s + LANES], sw_ref[...]) for s in range(0, n, LANES)], axis=1)
        o_ref[...] = y
        off += n


def norm_matmul(x, nw, w, splits, slot_norm_w=None, tile=TOK_TILE, name="norm_matmul"):
    t, d = x.shape
    n = w[0].shape[1]
    assert sum(splits) == n and t % tile == 0
    in_specs = [pl.BlockSpec((tile, d), lambda i: (i, 0)), pl.BlockSpec((1, d), lambda i: (0, 0))]
    in_specs += [_full(a, 1) for a in w]
    args = [x, nw.reshape(1, d)] + list(w)
    if slot_norm_w is not None:
        in_specs.append(pl.BlockSpec((1, LANES), lambda i: (0, 0)))
        args.append(slot_norm_w.reshape(1, LANES))
    return pl.pallas_call(
        functools.partial(_norm_matmul_kernel, splits=splits, slot_norm=slot_norm_w is not None, nw=len(w)),
        grid=(t // tile,),
        in_specs=in_specs,
        out_specs=[pl.BlockSpec((tile, s), lambda i: (i, 0)) for s in splits],
        out_shape=[jax.ShapeDtypeStruct((t, s), F32) for s in splits],
        compiler_params=_cparams("parallel"),
        name=name,
    )(*args)


def _pool_kernel(u_ref, sc_ref, *rest, tl, nw):
    w_refs, (o_ref, st_ref, hist_ref) = rest[:nw], rest[nw:]
    i = pl.program_id(1)

    @pl.when(i == 0)
    def _():
        hist_ref[...] = jnp.zeros_like(hist_ref)

    u = u_ref[...]
    ext = jnp.concatenate([hist_ref[...], u], axis=0)
    s2 = ext + pltpu.roll(ext, 1, 0)
    s4 = s2 + pltpu.roll(s2, 2, 0)
    s8 = s4 + pltpu.roll(s4, 4, 0)
    s16 = s8 + pltpu.roll(s8, 8, 0)
    lane = _lane((tl, GROUP_W))
    grp = lane // 64
    pooled = jnp.where(grp == 0, s2[16:], jnp.where(grp == 1, s4[16:], jnp.where(grp == 2, s8[16:], s16[16:])))
    win = jnp.where(grp == 0, 2, jnp.where(grp == 1, 4, jnp.where(grp == 2, 8, 16)))
    pos = i * tl + _row((tl, GROUP_W))
    cnt = jnp.minimum(win, pos + 1).astype(F32)
    d = pooled / cnt - u
    o_ref[...] = _mm(_act(d, nw == 2), _wts(w_refs)) * sc_ref[...]
    hist_ref[...] = ext[tl:]
    st_ref[0] = ext[tl:]


def pool_prompt(z_a, w_bd, scale, n_seq, seq_len, tl=512):
    nt = seq_len // tl
    return pl.pallas_call(
        functools.partial(_pool_kernel, tl=tl, nw=len(w_bd)),
        grid=(n_seq, nt),
        in_specs=[pl.BlockSpec((tl, GROUP_W), lambda s, i: (s * nt + i, 0)),
                  pl.BlockSpec((1, GROUP_W), lambda s, i: (0, 0))] + [_full(a, 2) for a in w_bd],
        out_specs=[pl.BlockSpec((tl, GROUP_W), lambda s, i: (s * nt + i, 0)),
                   pl.BlockSpec((1, 16, GROUP_W), lambda s, i: (s, 0, 0))],
        out_shape=[jax.ShapeDtypeStruct((n_seq * seq_len, GROUP_W), F32),
                   jax.ShapeDtypeStruct((n_seq, 16, GROUP_W), F32)],
        scratch_shapes=[pltpu.VMEM((16, GROUP_W), F32)],
        compiler_params=_cparams("parallel", "arbitrary"),
        name="pool_prompt",
    )(z_a, scale.reshape(1, GROUP_W), *w_bd)


def _conv_kernel(u_ref, cw_ref, cb_ref, lw_ref, lb_ref, pb_ref, *rest, tl, nw):
    w_refs, (o_ref, st_ref, ext_ref) = rest[:nw], rest[nw:]
    i = pl.program_id(1)

    @pl.when(i == 0)
    def _():
        ext_ref[0:32, :] = jnp.zeros((32, CONV_C), F32)

    u = u_ref[...]
    g = u[:, :CONV_C] * jax.nn.sigmoid(u[:, CONV_C:])
    ext_ref[32:, :] = g
    y = jnp.zeros((tl, CONV_C), F32)
    for j in range(CONV_W):
        y = y + ext_ref[2 + j:2 + j + tl, :] * cw_ref[j:j + 1, :]
    y = y + cb_ref[...]
    mu = jnp.mean(y, axis=-1, keepdims=True)
    yc = y - mu
    var = jnp.mean(yc * yc, axis=-1, keepdims=True)
    y = yc * lax.rsqrt(var + EPS) * lw_ref[...] + lb_ref[...]
    o_ref[...] = _mm(_act(jax.nn.silu(y), nw == 2), _wts(w_refs)) + pb_ref[...]
    tail = ext_ref[tl:tl + 32, :]
    st_ref[0] = tail
    ext_ref[0:32, :] = tail


def conv_prompt(z_d, cw, cb, lw, lb, pw, pb, n_seq, seq_len, tl=512):
    nt = seq_len // tl
    vec = lambda: pl.BlockSpec((1, CONV_C), lambda s, i: (0, 0))
    return pl.pallas_call(
        functools.partial(_conv_kernel, tl=tl, nw=len(pw)),
        grid=(n_seq, nt),
        in_specs=[pl.BlockSpec((tl, 2 * CONV_C), lambda s, i: (s * nt + i, 0)),
                  pl.BlockSpec((32, CONV_C), lambda s, i: (0, 0)),
                  vec(), vec(), vec(), vec()] + [_full(a, 2) for a in pw],
        out_specs=[pl.BlockSpec((tl, CONV_C), lambda s, i: (s * nt + i, 0)),
                   pl.BlockSpec((1, 32, CONV_C), lambda s, i: (s, 0, 0))],
        out_shape=[jax.ShapeDtypeStruct((n_seq * seq_len, CONV_C), F32),
                   jax.ShapeDtypeStruct((n_seq, 32, CONV_C), F32)],
        scratch_shapes=[pltpu.VMEM((tl + 32, CONV_C), F32)],
        compiler_params=_cparams("parallel", "arbitrary"),
        name="conv_prompt",
    )(z_d, cw, cb.reshape(1, -1), lw.reshape(1, -1), lb.reshape(1, -1), pb.reshape(1, -1), *pw)


def _split_pair(x):
    low = _lane(x.shape) < 64
    return jnp.where(low, x, 0.0), jnp.where(low, pltpu.roll(x, 64, 1), 0.0)


def _hgrn_chunk(zc, lb, onw, tri, s_ref, hp):
    c = zc.shape[0]
    nk = HG_HEADS * HG_DK
    q = zc[:, 0:nk]
    zf = zc[:, nk:2 * nk]
    v_all = zc[:, 2 * nk:2 * nk + 256]
    g_all = zc[:, 2 * nk + 256:2 * nk + 512]
    log_f = jnp.log(lb + (1.0 - lb) * jax.nn.sigmoid(zf))
    k = (1.0 - lb) * jax.nn.sigmoid(-zf)
    hi = log_f.astype(BF16)
    r1 = log_f - hi.astype(F32)
    mid = r1.astype(BF16)
    lo = (r1 - mid.astype(F32)).astype(BF16)
    b = _dot(tri, hi) + _dot(tri, mid) + _dot(tri, lo)
    b_last = b[c - 1:c, :]
    q_inter = q * jnp.exp(b)
    k_tail = k * jnp.exp(b_last - b)
    decay_last = jnp.exp(b_last)
    n_sub = c // HG_SUB
    rowc = _row((HG_SUB, LANES))
    outs = []
    for p in range(2):
        v_pair = _split_pair(v_all[:, LANES * p:LANES * (p + 1)])
        g_pair = _split_pair(g_all[:, LANES * p:LANES * (p + 1)])
        o_pair = []
        for hh in range(2):
            h = 2 * p + hh
            sl = slice(HG_DK * h, HG_DK * (h + 1))
            vh = v_pair[hh]
            vh_t = _act(vh, hp)
            s_old = s_ref[h]
            o = _mm(_act(q_inter[:, sl], hp), _act(s_old, hp))
            bh, qh, kh = b[:, sl], q[:, sl], k[:, sl]
            parts = []
            for ib in range(n_sub):
                r0 = ib * HG_SUB
                b_i, q_i = bh[r0:r0 + HG_SUB], qh[r0:r0 + HG_SUB]
                acc = o[r0:r0 + HG_SUB]
                if ib > 0:
                    ref_b = bh[r0 - 1:r0, :]
                    qt = q_i * jnp.exp(b_i - ref_b)
                    kt = kh[0:r0] * jnp.exp(ref_b - bh[0:r0])
                    a = _mm(_act(qt, hp), _act(kt, hp), nt=True)
                    acc = acc + _mm(_act(a, hp), tuple(t[0:r0] for t in vh_t))
                for s in range(HG_SUB):
                    r = r0 + s
                    e = jnp.exp(jnp.minimum(b_i - bh[r:r + 1, :], 0.0))
                    pr = jnp.where(rowc >= s, q_i * kh[r:r + 1, :] * e, 0.0)
                    acc = acc + jnp.sum(pr, axis=-1, keepdims=True) * vh[r:r + 1, :]
                parts.append(acc)
            o = jnp.concatenate(parts, axis=0) if n_sub > 1 else parts[0]
            dcol = jnp.broadcast_to(decay_last[:, sl], (LANES, LANES)).T
            s_ref[h] = dcol * s_old + _mm(_act(k_tail[:, sl].T, hp), vh_t)
            o = _rms(o, onw, HG_DV) * jax.nn.silu(g_pair[hh])
            o_pair.append(o)
        outs.append(o_pair[0] + pltpu.roll(o_pair[1], 64, 1))
    return jnp.concatenate(outs, axis=1)


def _hgrn_kernel(z_ref, lb_ref, onw_ref, tri_ref, o_ref, st_ref, s_ref, *, tl, hp):
    i = pl.program_id(1)

    @pl.when(i == 0)
    def _():
        s_ref[...] = jnp.zeros_like(s_ref)

    def body(cidx, carry):
        r0 = pl.multiple_of(cidx * HG_CHUNK, HG_CHUNK)
        zc = z_ref[pl.ds(r0, HG_CHUNK), :]
        o_ref[pl.ds(r0, HG_CHUNK), :] = _hgrn_chunk(zc, lb_ref[...], onw_ref[...], tri_ref[...], s_ref, hp)
        return carry

    lax.fori_loop(0, tl // HG_CHUNK, body, 0)
    st_ref[0] = s_ref[...]


def hgrn_prompt(z_c, lb, onw_slot, n_seq, seq_len, hp, tl=512):
    nt = seq_len // tl
    tri = jnp.tril(jnp.ones((HG_CHUNK, HG_CHUNK), F32)).astype(BF16)
    zw = z_c.shape[1]
    return pl.pallas_call(
        functools.partial(_hgrn_kernel, tl=tl, hp=hp),
        grid=(n_seq, nt),
        in_specs=[pl.BlockSpec((tl, zw), lambda s, i: (s * nt + i, 0)),
                  pl.BlockSpec((1, HG_HEADS * HG_DK), lambda s, i: (0, 0)),
                  pl.BlockSpec((1, LANES), lambda s, i: (0, 0)),
                  pl.BlockSpec((HG_CHUNK, HG_CHUNK), lambda s, i: (0, 0))],
        out_specs=[pl.BlockSpec((tl, GROUP_W), lambda s, i: (s * nt + i, 0)),
                   pl.BlockSpec((1, HG_HEADS, HG_DK, LANES), lambda s, i: (s, 0, 0, 0))],
        out_shape=[jax.ShapeDtypeStruct((n_seq * seq_len, GROUP_W), F32),
                   jax.ShapeDtypeStruct((n_seq, HG_HEADS, HG_DK, LANES), F32)],
        scratch_shapes=[pltpu.VMEM((HG_HEADS, HG_DK, LANES), F32)],
        compiler_params=_cparams("parallel", "arbitrary"),
        name="hgrn_prompt",
    )(z_c, lb.reshape(1, -1), onw_slot.reshape(1, LANES), tri)


def _mla_proj_kernel(z_ref, cs_ref, sn_ref, qnw_ref, qv_ref, qvs_ref, kvnw_ref, kv_ref, krv_ref, krvs_ref,
                     *rest, absorb, nw):
    wq_refs, wkv_refs, rest = rest[:nw], rest[nw:2 * nw], rest[2 * nw:]
    if absorb:
        wabs_ref, q_ref, k_ref, v_ref, ckv_ref, kr_ref, qabs_ref = rest
    else:
        q_ref, k_ref, v_ref, ckv_ref, kr_ref = rest
    hp = nw == 2
    z = z_ref[...]
    cos, sin = cs_ref[...], sn_ref[...]
    tm = z.shape[0]
    lane = _lane((tm, LANES))
    is_rope = lane < MLA_ROPE
    kr_raw = z[:, 384:512]
    kr_sw = z[:, 512:640]
    r_kr = lax.rsqrt(jnp.sum(kr_raw * kr_raw, axis=-1, keepdims=True) * (1.0 / MLA_ROPE) + EPS)
    k_rope = (kr_raw * r_kr * krv_ref[...]) * cos + (kr_sw * r_kr * krvs_ref[...]) * sin
    kr_ref[...] = k_rope
    c_kv = _rms(z[:, 256:384], kvnw_ref[...])
    ckv_ref[...] = c_kv
    kv = _mm(_act(c_kv, hp), _wts(wkv_refs))
    v_ref[...] = kv[:, 512:].astype(v_ref.dtype)
    ks = []
    for h in range(MLA_HEADS):
        kraw = kv[:, LANES * h:LANES * (h + 1)]
        ks.append(_rms(kraw, kv_ref[...], MLA_NOPE) + k_rope)
    k_ref[...] = jnp.concatenate(ks, axis=1).astype(k_ref.dtype)
    c_q = _rms(z[:, 0:256], qnw_ref[...])
    qq = _mm(_act(c_q, hp), _wts(wq_refs))
    qs = []
    for h in range(MLA_HEADS):
        x = qq[:, LANES * h:LANES * (h + 1)]
        xs = qq[:, 512 + LANES * h:512 + LANES * (h + 1)]
        x2 = x * x
        r_rope = lax.rsqrt(jnp.sum(jnp.where(is_rope, x2, 0.0), axis=-1, keepdims=True) * (1.0 / MLA_ROPE) + EPS)
        r_nope = lax.rsqrt(jnp.sum(jnp.where(is_rope, 0.0, x2), axis=-1, keepdims=True) * (1.0 / MLA_NOPE) + EPS)
        y = x * jnp.where(is_rope, r_rope, r_nope) * qv_ref[...]
        ysw = xs * r_rope * qvs_ref[...]
        qs.append((y * cos + ysw * sin) * MLA_SCALE)
    q = jnp.concatenate(qs, axis=1)
    q_ref[...] = q.astype(q_ref.dtype)
    if absorb:
        kvec4 = jnp.concatenate([kv_ref[...]] * MLA_HEADS, axis=1)
        qabs_ref[...] = _dot((q * kvec4).astype(BF16), wabs_ref[...])


def mla_proj(z_b, cos_t, sin_t, pw, row0, n_rows, absorb, hp, tile=TOK_TILE):
    assert row0 % tile == 0 and n_rows % tile == 0
    t0 = row0 // tile
    rows = lambda w: pl.BlockSpec((tile, w), lambda i: (i + t0, 0))
    out_rows = lambda w: pl.BlockSpec((tile, w), lambda i: (i, 0))
    consts = [pw["q_norm_w"], pw["qvec"], pw["qvec_sw"], pw["kv_norm_w"], pw["kvec"], pw["krvec"], pw["krvec_sw"]]
    consts += pw["wq"] + pw["wkv"]
    if absorb:
        consts.append(pw["wabs"])
    qkv_dt = F32 if hp else BF16
    out_shape = [jax.ShapeDtypeStruct((n_rows, 512), qkv_dt)] * 3 + [jax.ShapeDtypeStruct((n_rows, LANES), F32)] * 2
    out_specs = [out_rows(512)] * 3 + [out_rows(LANES)] * 2
    if absorb:
        out_shape.append(jax.ShapeDtypeStruct((n_rows, 512), F32))
        out_specs.append(out_rows(512))
    return pl.pallas_call(
        functools.partial(_mla_proj_kernel, absorb=absorb, nw=len(pw["wq"])),
        grid=(n_rows // tile,),
        in_specs=[rows(640), rows(LANES), rows(LANES)] + [_full(a, 1) for a in consts],
        out_specs=out_specs,
        out_shape=out_shape,
        compiler_params=_cparams("parallel"),
        name="mla_proj_s" if absorb else "mla_proj_p",
    )(z_b, cos_t, sin_t, *consts)


def _flash_kernel(qi_ref, kj_ref, q_ref, k_ref, v_ref, o_ref, m_ref, l_ref, acc_ref, *, tq, hp):
    n = pl.program_id(1)
    qi, kj = qi_ref[n], kj_ref[n]

    @pl.when(kj == 0)
    def _():
        m_ref[...] = jnp.full_like(m_ref, -jnp.inf)
        l_ref[...] = jnp.zeros_like(l_ref)
        acc_ref[...] = jnp.zeros_like(acc_ref)

    def operand(ref, sl):
        x = ref[:, sl]
        return _split(x) if hp else (x,)

    def step(masked):
        for h in range(MLA_HEADS):
            sl = slice(LANES * h, LANES * (h + 1))
            s = _mm(operand(q_ref, sl), operand(k_ref, sl), nt=True)
            if masked:
                s = jnp.where(_row((tq, tq)) >= _lane((tq, tq)), s, -jnp.inf)
            m_prev = m_ref[h]
            m_new = jnp.maximum(m_prev, jnp.max(s, axis=-1, keepdims=True))
            alpha = jnp.exp(m_prev - m_new)
            p = jnp.exp(s - jnp.tile(m_new, (1, tq // LANES)))
            l_ref[h] = alpha * l_ref[h] + jnp.sum(p, axis=-1, keepdims=True)
            acc_ref[h] = alpha * acc_ref[h] + _mm(_act(p, hp), operand(v_ref, sl))
            m_ref[h] = m_new

    @pl.when(kj < qi)
    def _():
        step(False)

    @pl.when(kj == qi)
    def _():
        step(True)
        o_ref[...] = jnp.concatenate([acc_ref[h] / l_ref[h] for h in range(MLA_HEADS)], axis=1)


def flash_prompt(q, k, v, n_seq, seq_len, hp, tq=512):
    nq = seq_len // tq
    qi = np.array([i for i in range(nq) for _ in range(i + 1)], np.int32)
    kj = np.array([j for i in range(nq) for j in range(i + 1)], np.int32)
    grid_spec = pltpu.PrefetchScalarGridSpec(
        num_scalar_prefetch=2,
        grid=(n_seq, len(qi)),
        in_specs=[pl.BlockSpec((tq, 512), lambda b, n, qi, kj: (b * nq + qi[n], 0)),
                  pl.BlockSpec((tq, 512), lambda b, n, qi, kj: (b * nq + kj[n], 0)),
                  pl.BlockSpec((tq, 512), lambda b, n, qi, kj: (b * nq + kj[n], 0))],
        out_specs=pl.BlockSpec((tq, 512), lambda b, n, qi, kj: (b * nq + qi[n], 0)),
        scratch_shapes=[pltpu.VMEM((MLA_HEADS, tq, LANES), F32), pltpu.VMEM((MLA_HEADS, tq, LANES), F32),
                        pltpu.VMEM((MLA_HEADS, tq, LANES), F32)])
    return pl.pallas_call(
        functools.partial(_flash_kernel, tq=tq, hp=hp),
        grid_spec=grid_spec,
        out_shape=jax.ShapeDtypeStruct((n_seq * seq_len, 512), F32),
        compiler_params=_cparams("parallel", "arbitrary"),
        name="mla_flash_prompt",
    )(jnp.asarray(qi), jnp.asarray(kj), q, k, v)


def _mla_sample_kernel(pt_ref, cache_c, cache_r, qa_ref, qr_ref, cn_ref, rn_ref, wukt_ref, o_ref,
                       m_ref, l_ref, acc_ref, cbuf, rbuf, csem, rsem, *, npg, n_steps, layer):
    g = pl.program_id(1)
    n = pl.program_id(0) * n_steps + g
    n_total = pl.num_programs(0) * n_steps
    slot = n % 2

    def page_copies(step, s, p):
        page = pt_ref[step * npg + p]
        return (pltpu.make_async_copy(cache_c.at[layer, page], cbuf.at[s, p], csem.at[s]),
                pltpu.make_async_copy(cache_r.at[layer, page], rbuf.at[s, p], rsem.at[s]))

    def start_pages(step, s):
        for p in range(npg):
            for cp in page_copies(step, s, p):
                cp.start()

    @pl.when(n == 0)
    def _():
        start_pages(0, 0)

    @pl.when(n + 1 < n_total)
    def _():
        start_pages(n + 1, 1 - slot)

    for p in range(npg):
        for cp in page_copies(n, slot, p):
            cp.wait()

    @pl.when(g == 0)
    def _():
        m_ref[...] = jnp.full_like(m_ref, -jnp.inf)
        l_ref[...] = jnp.zeros_like(l_ref)
        acc_ref[...] = jnp.zeros_like(acc_ref)

    qa = qa_ref[...].astype(BF16)
    qr = qr_ref[...][:, :MLA_ROPE].astype(BF16)
    nk = MLA_HEADS * MLA_NOPE
    lhs = jnp.concatenate([wukt_ref[...], qa], axis=0)

    def nope_scores(cb):
        big = _dot_nt(lhs, cb)
        kraw_t = big[:nk]
        sq = kraw_t * kraw_t
        rk = [lax.rsqrt(jnp.sum(sq[MLA_NOPE * h:MLA_NOPE * (h + 1)], axis=0, keepdims=True) * (1.0 / MLA_NOPE) + EPS)
              for h in range(MLA_HEADS)]
        rk_rows = jnp.concatenate(rk * (qa.shape[0] // MLA_HEADS), axis=0)
        return big[nk:] * rk_rows

    def update(s_all, cb):
        m_prev = m_ref[...]
        m_new = jnp.maximum(m_prev, jnp.max(s_all, axis=-1, keepdims=True))
        alpha = jnp.exp(m_prev - m_new)
        p_all = jnp.exp(s_all - m_new)
        l_ref[...] = alpha * l_ref[...] + jnp.sum(p_all, axis=-1, keepdims=True)
        acc_ref[...] = alpha * acc_ref[...] + _dot(p_all.astype(BF16), cb)
        m_ref[...] = m_new

    cb_all = jnp.concatenate([cbuf[slot, p].astype(BF16) for p in range(npg)], axis=0)
    kr_all = jnp.concatenate([rbuf[slot, p].astype(BF16) for p in range(npg)], axis=1)
    update(nope_scores(cb_all) + _dot(qr, kr_all), cb_all)

    @pl.when(g == n_steps - 1)
    def _():
        cb = cn_ref[...].astype(BF16)
        s_new = nope_scores(cb) + _dot_nt(qr, rn_ref[...][:, :MLA_ROPE].astype(BF16))
        qtok = _row(s_new.shape) // MLA_HEADS
        s_new = jnp.where(_lane(s_new.shape) <= qtok, s_new, -jnp.inf)
        update(s_new, cb)
        o_ref[...] = acc_ref[...] / l_ref[...]


def mla_sample(page_table, cache_c, cache_rt, layer, qa, qr, c_new, r_new, wukt):
    nb, n_pages = page_table.shape
    npg = PAGES_PER_STEP
    n_steps = n_pages // npg
    rows = qa.shape[1]
    s_pad = c_new.shape[1]

    per_seq = lambda r, w: pl.BlockSpec((None, r, w), lambda b, g, pt: (b, 0, 0))
    full = lambda a: pl.BlockSpec(a.shape, lambda b, g, pt: (0,) * a.ndim)
    any_spec = pl.BlockSpec(memory_space=pl.ANY)
    grid_spec = pltpu.PrefetchScalarGridSpec(
        num_scalar_prefetch=1,
        grid=(nb, n_steps),
        in_specs=[any_spec, any_spec,
                  per_seq(rows, LANES), per_seq(rows, LANES), per_seq(s_pad, LANES), per_seq(s_pad, LANES),
                  full(wukt)],
        out_specs=per_seq(rows, LANES),
        scratch_shapes=[pltpu.VMEM((rows, 1), F32), pltpu.VMEM((rows, 1), F32), pltpu.VMEM((rows, LANES), F32),
                        pltpu.VMEM((2, npg, PAGE_SIZE, MLA_KV_LORA), F32),
                        pltpu.VMEM((2, npg, MLA_ROPE, PAGE_SIZE), F32),
                        pltpu.SemaphoreType.DMA((2,)), pltpu.SemaphoreType.DMA((2,))])
    return pl.pallas_call(
        functools.partial(_mla_sample_kernel, npg=npg, n_steps=n_steps, layer=layer),
        grid_spec=grid_spec,
        out_shape=jax.ShapeDtypeStruct((nb, rows, LANES), F32),
        compiler_params=_cparams("arbitrary", "arbitrary"),
        name="mla_sample",
    )(page_table.reshape(-1), cache_c, cache_rt, qa, qr, c_new, r_new, wukt)


def _matmul_kernel(a_ref, w_ref, o_ref):
    o_ref[...] = _dot(a_ref[...].astype(BF16), w_ref[...])


def matmul(a, w, tile=TOK_TILE, name="matmul"):
    t, kdim = a.shape
    n = w.shape[1]
    return pl.pallas_call(
        _matmul_kernel,
        grid=(t // tile,),
        in_specs=[pl.BlockSpec((tile, kdim), lambda i: (i, 0)), pl.BlockSpec((kdim, n), lambda i: (0, 0))],
        out_specs=pl.BlockSpec((tile, n), lambda i: (i, 0)),
        out_shape=jax.ShapeDtypeStruct((t, n), F32),
        compiler_params=_cparams("parallel"),
        name=name,
    )(a, w)


def _mix_out_kernel(*refs, nw, n_prompt_tiles):
    p_refs, s_refs, x_ref, g_refs, n2_ref = refs[0:4], refs[4:8], refs[8], refs[9:13], refs[13]
    rest = refs[14:]
    w_groups = [rest[nw * g:nw * (g + 1)] for g in range(5)]
    x1_ref, q_ref = rest[5 * nw:]
    hp = nw == 2
    is_prompt = pl.program_id(0) < n_prompt_tiles
    acc = x_ref[...]
    for p_ref, s_ref, g_ref, w_refs in zip(p_refs, s_refs, g_refs, w_groups[:4]):
        o = jnp.where(is_prompt, p_ref[...], s_ref[...])
        acc = acc + _mm(_act(_rms(o, g_ref[...], GROUP_W), hp), _wts(w_refs))
    x1_ref[...] = acc
    q_ref[...] = _mm(_act(_rms(acc, n2_ref[...]), hp), _wts(w_groups[4]))


def mix_out(o_prompt, o_sample, x, gw, w_out, n2w, wq, tile=TOK_TILE):
    t = x.shape[0]
    npt = o_prompt[0].shape[0] // tile
    assert all(a.shape[0] == tile for a in o_sample) and t == (npt + 1) * tile
    prompt_rows = lambda a: pl.BlockSpec((tile, a.shape[1]), lambda i: (jnp.minimum(i, npt - 1), 0))
    acts = list(o_prompt) + list(o_sample) + [x]
    consts = list(gw) + [n2w.reshape(1, -1)] + [a for w in w_out for a in w] + list(wq)
    return pl.pallas_call(
        functools.partial(_mix_out_kernel, nw=len(wq), n_prompt_tiles=npt),
        grid=(t // tile,),
        in_specs=[prompt_rows(a) for a in o_prompt] + [_full(a, 1) for a in o_sample]
                 + [pl.BlockSpec((tile, D_MODEL), lambda i: (i, 0))] + [_full(a, 1) for a in consts],
        out_specs=[pl.BlockSpec((tile, D_MODEL), lambda i: (i, 0)), pl.BlockSpec((tile, CA_W), lambda i: (i, 0))],
        out_shape=[jax.ShapeDtypeStruct((t, D_MODEL), F32), jax.ShapeDtypeStruct((t, CA_W), F32)],
        compiler_params=_cparams("parallel"),
        name="mix_out",
    )(*acts, *consts)


def _cross_kernel(q_ref, mk_ref, mv_ref, x_ref, qn_ref, *rest, nw):
    wo_refs, (o_ref,) = rest[:nw], rest[nw:]
    hp = nw == 2
    q = q_ref[...]
    outs = []
    for h in range(CA_HEADS):
        sl = slice(CA_HD * h, CA_HD * (h + 1))
        qh = _rms(q[:, sl], qn_ref[...])
        s = _mm(_act(qh, hp), _act(mk_ref[:, sl], hp), nt=True) * (CA_HD ** -0.5)
        s = s - jnp.max(s, axis=-1, keepdims=True)
        e = jnp.exp(s)
        p = e / jnp.sum(e, axis=-1, keepdims=True)
        outs.append(_mm(_act(p, hp), _act(mv_ref[:, sl], hp)))
    o = jnp.concatenate(outs, axis=1)
    o_ref[...] = x_ref[...] + _mm(_act(o, hp), _wts(wo_refs))


def cross_attend(q, mem_k, mem_v, x, qnw, wo, n_seq, seq_len, tl):
    nt = seq_len // tl
    mem_spec = pl.BlockSpec((None, N_MEM, CA_W), lambda s, i: (s, 0, 0))
    return pl.pallas_call(
        functools.partial(_cross_kernel, nw=len(wo)),
        grid=(n_seq, nt),
        in_specs=[pl.BlockSpec((tl, CA_W), lambda s, i: (s * nt + i, 0)),
                  mem_spec, mem_spec,
                  pl.BlockSpec((tl, D_MODEL), lambda s, i: (s * nt + i, 0)),
                  pl.BlockSpec((1, CA_HD), lambda s, i: (0, 0))] + [_full(a, 2) for a in wo],
        out_specs=pl.BlockSpec((tl, D_MODEL), lambda s, i: (s * nt + i, 0)),
        out_shape=jax.ShapeDtypeStruct((n_seq * seq_len, D_MODEL), F32),
        compiler_params=_cparams("parallel", "parallel"),
        name="cross_attend",
    )(q, mem_k, mem_v, x, qnw.reshape(1, CA_HD), *wo)


SAMPLE_SEQS_PER_STEP = 8


def _cross_sample_kernel(q_ref, mk_ref, mv_ref, x_ref, qn_ref, *rest, nw, s_len):
    wo_refs, (o_ref,) = rest[:nw], rest[nw:]
    hp = nw == 2
    q = q_ref[...]
    rows = CA_HEADS * s_len
    shape = (rows, N_MEM * CA_HEADS)
    own_head = (_lane(shape) % CA_HEADS) == (_row(shape) // s_len)
    outs = []
    for s in range(SAMPLE_SEQS_PER_STEP):
        qs = q[s * s_len:(s + 1) * s_len]
        qh = jnp.concatenate([qs[:, CA_HD * h:CA_HD * (h + 1)] for h in range(CA_HEADS)], axis=0)
        qn = _rms(qh, qn_ref[...])
        sc = _mm(_act(qn, hp), _act(mk_ref[s], hp), nt=True) * (CA_HD ** -0.5)
        sc = jnp.where(own_head, sc, -jnp.inf)
        sc = sc - jnp.max(sc, axis=-1, keepdims=True)
        e = jnp.exp(sc)
        p = e / jnp.sum(e, axis=-1, keepdims=True)
        o = _mm(_act(p, hp), _act(mv_ref[s], hp))
        outs.append(jnp.concatenate([o[h * s_len:(h + 1) * s_len] for h in range(CA_HEADS)], axis=1))
    o_all = jnp.concatenate(outs, axis=0)
    o_ref[...] = x_ref[...] + _mm(_act(o_all, hp), _wts(wo_refs))


def cross_attend_sample(q, mem_k, mem_v, x, qnw, wo, s_len, layer):
    sb = SAMPLE_SEQS_PER_STEP
    n_seq = q.shape[0] // s_len
    rows = sb * s_len
    mem_spec = pl.BlockSpec((None, sb, N_MEM * CA_HEADS, CA_HD), lambda i: (layer, i, 0, 0))
    return pl.pallas_call(
        functools.partial(_cross_sample_kernel, nw=len(wo), s_len=s_len),
        grid=(n_seq // sb,),
        in_specs=[pl.BlockSpec((rows, CA_W), lambda i: (i, 0)), mem_spec, mem_spec,
                  pl.BlockSpec((rows, D_MODEL), lambda i: (i, 0)),
                  pl.BlockSpec((1, CA_HD), lambda i: (0, 0))] + [_full(a, 1) for a in wo],
        out_specs=pl.BlockSpec((rows, D_MODEL), lambda i: (i, 0)),
        out_shape=jax.ShapeDtypeStruct((n_seq * s_len, D_MODEL), F32),
        compiler_params=_cparams("parallel"),
        name="cross_attend_sample",
    )(q, mem_k, mem_v, x, qnw.reshape(1, CA_HD), *wo)


def _swiglu_into(h, wg_refs, wu_refs, wd_refs, o_ref, width):
    hp = len(wg_refs) == 2
    for c0 in range(0, width, FF_CHUNK):
        cols = (slice(None), slice(c0, c0 + FF_CHUNK))
        g = _mm(h, _wts(wg_refs, cols))
        u = _mm(h, _wts(wu_refs, cols))
        a = _act(jax.nn.silu(g) * u, hp)
        o_ref[...] += _mm(a, _wts(wd_refs, (slice(c0, c0 + FF_CHUNK), slice(None))))


def _ffn_kernel(x_ref, nw_ref, *rest, nw, width):
    wg, wu, wd, (o_ref,) = rest[:nw], rest[nw:2 * nw], rest[2 * nw:3 * nw], rest[3 * nw:]
    x = x_ref[...]

    @pl.when(pl.program_id(1) == 0)
    def _():
        o_ref[...] = x

    _swiglu_into(_act(_rms(x, nw_ref[...]), nw == 2), wg, wu, wd, o_ref, width)


def ffn_dense(x, nw, wg, wu, wd, ff_block, tile=TOK_TILE):
    t = x.shape[0]
    up = lambda: pl.BlockSpec((D_MODEL, ff_block), lambda i, j: (0, j))
    down = lambda: pl.BlockSpec((ff_block, D_MODEL), lambda i, j: (j, 0))
    n = len(wg)
    return pl.pallas_call(
        functools.partial(_ffn_kernel, nw=n, width=ff_block),
        grid=(t // tile, D_FF // ff_block),
        in_specs=[pl.BlockSpec((tile, D_MODEL), lambda i, j: (i, 0)), pl.BlockSpec((1, D_MODEL), lambda i, j: (0, 0))]
                 + [up() for _ in range(2 * n)] + [down() for _ in range(n)],
        out_specs=pl.BlockSpec((tile, D_MODEL), lambda i, j: (i, 0)),
        out_shape=jax.ShapeDtypeStruct((t, D_MODEL), F32),
        compiler_params=_cparams("parallel", "arbitrary"),
        name="ffn_dense",
    )(x, nw.reshape(1, -1), *wg, *wu, *wd)


def _router_kernel(x_ref, nw_ref, r_ref, h_ref, g_ref):
    h = _rms(x_ref[...], nw_ref[...])
    h_ref[...] = h
    logits = jnp.dot(h, r_ref[...], preferred_element_type=F32, precision=lax.Precision.HIGHEST)
    lane = _lane(logits.shape)
    neg = -jnp.inf
    l1 = jnp.where(lane < N_EXPERTS, logits, neg)
    m1 = jnp.max(l1, axis=-1, keepdims=True)
    i1 = jnp.min(jnp.where(l1 == m1, lane, LANES), axis=-1, keepdims=True)
    l2 = jnp.where(lane == i1, neg, l1)
    m2 = jnp.max(l2, axis=-1, keepdims=True)
    i2 = jnp.min(jnp.where(l2 == m2, lane, LANES), axis=-1, keepdims=True)
    e2 = jnp.exp(m2 - m1)
    den = 1.0 + e2
    gates = jnp.where(lane == i1, 1.0 / den, jnp.where(lane == i2, e2 / den, 0.0))
    g_ref[...] = jnp.where(lane == N_EXPERTS, i1.astype(F32), jnp.where(lane == N_EXPERTS + 1, i2.astype(F32), gates))


def moe_router(x, nw, router_pad, tile=TOK_TILE):
    t = x.shape[0]
    return pl.pallas_call(
        _router_kernel,
        grid=(t // tile,),
        in_specs=[pl.BlockSpec((tile, D_MODEL), lambda i: (i, 0)), pl.BlockSpec((1, D_MODEL), lambda i: (0, 0)),
                  pl.BlockSpec((D_MODEL, LANES), lambda i: (0, 0))],
        out_specs=[pl.BlockSpec((tile, D_MODEL), lambda i: (i, 0)), pl.BlockSpec((tile, LANES), lambda i: (i, 0))],
        out_shape=[jax.ShapeDtypeStruct((t, D_MODEL), F32), jax.ShapeDtypeStruct((t, LANES), F32)],
        compiler_params=_cparams("parallel"),
        name="moe_router",
    )(x, nw.reshape(1, -1), router_pad)


def _experts_kernel(be_ref, nv_ref, src_ref, nxt_ref, h_hbm, wg_ref, wu_ref, wd_ref, o_ref, xbuf, sem, *, tile):
    b = pl.program_id(0)
    slot = b % 2
    nv = nv_ref[0]

    def row_copy(idx_ref, i, s):
        return pltpu.make_async_copy(h_hbm.at[pl.ds(idx_ref[0, i], 1), :], xbuf.at[s, pl.ds(i, 1), :], sem.at[s])

    def start_rows(idx_ref, s):
        def body(i, c):
            row_copy(idx_ref, i, s).start()
            return c
        lax.fori_loop(0, tile, body, 0, unroll=8)

    @pl.when((b == 0) & (nv > 0))
    def _():
        start_rows(src_ref, 0)

    @pl.when(b + 1 < nv)
    def _():
        start_rows(nxt_ref, 1 - slot)

    o_ref[...] = jnp.zeros_like(o_ref)

    @pl.when(b < nv)
    def _():
        def body(i, c):
            row_copy(src_ref, i, slot).wait()
            return c
        lax.fori_loop(0, tile, body, 0, unroll=8)
        _swiglu_into((xbuf[slot].astype(BF16),), (wg_ref,), (wu_ref,), (wd_ref,), o_ref, D_FF)


def moe_experts(h, src, blk_e, n_valid, wg, wu, wd, j, tile=MOE_TILE):
    n_blk = src.shape[0]
    wspec = lambda a: pl.BlockSpec((None, None) + a.shape[2:], lambda b, be, nv: (j, be[b], 0, 0))
    idx_spec = lambda f: pl.BlockSpec((None, 1, tile), f, memory_space=pltpu.SMEM)
    grid_spec = pltpu.PrefetchScalarGridSpec(
        num_scalar_prefetch=2,
        grid=(n_blk,),
        in_specs=[idx_spec(lambda b, be, nv: (b, 0, 0)),
                  idx_spec(lambda b, be, nv: (jnp.minimum(b + 1, n_blk - 1), 0, 0)),
                  pl.BlockSpec(memory_space=pl.ANY), wspec(wg), wspec(wu), wspec(wd)],
        out_specs=pl.BlockSpec((tile, D_MODEL), lambda b, be, nv: (b, 0)),
        scratch_shapes=[pltpu.VMEM((2, tile, D_MODEL), F32), pltpu.SemaphoreType.DMA((2,))])
    return pl.pallas_call(
        functools.partial(_experts_kernel, tile=tile),
        grid_spec=grid_spec,
        out_shape=jax.ShapeDtypeStruct((n_blk * tile, D_MODEL), F32),
        compiler_params=_cparams("arbitrary"),
        name="moe_experts",
    )(blk_e, n_valid, src, src, h, wg, wu, wd)


def moe_ffn(x, nw, router_pad, wg, wu, wd, j):
    t = x.shape[0]
    h, gfull = moe_router(x, nw, router_pad)
    e_idx = gfull[:, N_EXPERTS:N_EXPERTS + 2].astype(jnp.int32)
    g2 = jnp.take_along_axis(gfull[:, :N_EXPERTS], e_idx, axis=1)
    sel = (jnp.arange(N_EXPERTS, dtype=jnp.int32)[None, :] == e_idx[:, 0:1]) | (
        jnp.arange(N_EXPERTS, dtype=jnp.int32)[None, :] == e_idx[:, 1:2])
    pos = jnp.cumsum(sel.astype(jnp.int32), axis=0) - 1
    counts = pos[-1] + 1
    nb_e = (counts + MOE_TILE - 1) // MOE_TILE
    blk_end = jnp.cumsum(nb_e)
    row_start = (blk_end - nb_e) * MOE_TILE
    slot = row_start[None, :] + pos
    assert (2 * t) % MOE_TILE == 0
    n_blk = (2 * t + N_EXPERTS * MOE_TILE) // MOE_TILE
    stride = t + MOE_TILE
    big = N_EXPERTS * stride
    tok = jnp.arange(t, dtype=jnp.int32)[:, None]
    real = e_idx * stride + tok
    pad_e = nb_e * MOE_TILE - counts
    fill = jnp.arange(MOE_TILE, dtype=jnp.int32)[None, :]
    dummy = jnp.where(fill < pad_e[:, None],
                      jnp.arange(N_EXPERTS, dtype=jnp.int32)[:, None] * stride + t + fill, big)
    keys = jnp.sort(jnp.concatenate([real.reshape(-1), dummy.reshape(-1)]))
    src = keys % stride
    src = jnp.where((src >= t) | (keys >= big), 0, src).reshape(n_blk, 1, MOE_TILE)
    blk_e = jnp.minimum(jnp.searchsorted(blk_end, jnp.arange(n_blk, dtype=jnp.int32), side="right"),
                        N_EXPERTS - 1).astype(jnp.int32)
    y_buf = moe_experts(h, src, blk_e, blk_end[-1:].astype(jnp.int32), wg, wu, wd, j)
    s2 = jnp.take_along_axis(slot, e_idx, axis=1)
    y = y_buf[s2[:, 0]] * g2[:, 0:1] + y_buf[s2[:, 1]] * g2[:, 1:2]
    return x + y


def _zeros(r, c):
    return jnp.zeros((r, c), F32)


def _prep_w_in(w, hp):
    d = w.shape[0]
    k_r = w[:, 640:672]
    kr_slot = jnp.concatenate([k_r, _zeros(d, 96)], axis=1)
    kr_sw = jnp.concatenate([k_r[:, 16:], k_r[:, :16], _zeros(d, 96)], axis=1)
    return _w(jnp.concatenate([w[:, 0:256], w[:, 256:640], kr_slot, kr_sw, w[:, 672:2208], w[:, 2208:2720]],
                              axis=1), hp)


W_IN_SPLITS = (256, 640, 1536, 512)


def _slot_vec(rope_w, nope_w):
    z = jnp.zeros((32,), F32)
    return jnp.concatenate([rope_w, nope_w, z]).reshape(1, LANES)


def _prep_mla(lw, hp):
    wq, wkv = lw["mla_w_uq"], lw["mla_w_ukv"]
    slots, sw = [], []
    for h in range(MLA_HEADS):
        nope = wq[:, 96 * h:96 * h + 64]
        rope = wq[:, 96 * h + 64:96 * h + 96]
        slots.append(jnp.concatenate([rope, nope, _zeros(MLA_Q_LORA, 32)], axis=1))
        sw.append(jnp.concatenate([rope[:, 16:], rope[:, :16], _zeros(MLA_Q_LORA, 96)], axis=1))
    kslots = [jnp.concatenate([_zeros(128, 32), wkv[:, 128 * h:128 * h + 64], _zeros(128, 32)], axis=1)
              for h in range(MLA_HEADS)]
    vslots = [jnp.concatenate([wkv[:, 128 * h + 64:128 * h + 128], _zeros(128, 64)], axis=1)
              for h in range(MLA_HEADS)]
    z96 = jnp.zeros((96,), F32)
    qr, kr = lw["mla_qn_rope_w"], lw["mla_kn_rope_w"]
    wabs = jnp.zeros((512, 512), F32)
    for h in range(MLA_HEADS):
        wabs = wabs.at[128 * h + 32:128 * h + 96, 128 * h:128 * h + 128].set(wkv[:, 128 * h:128 * h + 64].T)
    wuv_bd = jnp.zeros((512, 256), F32)
    for h in range(MLA_HEADS):
        wuv_bd = wuv_bd.at[128 * h:128 * h + 128, 64 * h:64 * h + 64].set(wkv[:, 128 * h + 64:128 * h + 128])
    return dict(
        q_norm_w=lw["mla_q_norm_w"].reshape(1, -1),
        wq=_w(jnp.concatenate(slots + sw, axis=1), hp),
        qvec=_slot_vec(qr, lw["mla_qn_nope_w"]),
        qvec_sw=jnp.concatenate([qr[16:], qr[:16], z96]).reshape(1, LANES),
        kv_norm_w=lw["mla_kv_norm_w"].reshape(1, -1),
        wkv=_w(jnp.concatenate(kslots + vslots, axis=1), hp),
        kvec=_slot_vec(jnp.zeros((32,), F32), lw["mla_kn_nope_w"]),
        krvec=jnp.concatenate([kr, z96]).reshape(1, LANES),
        krvec_sw=jnp.concatenate([kr[16:], kr[:16], z96]).reshape(1, LANES),
        wabs=wabs.astype(BF16),
        wukt=jnp.concatenate([wkv[:, 128 * h:128 * h + 64] for h in range(MLA_HEADS)], axis=1).T.astype(BF16),
        wuv_bd=wuv_bd.astype(BF16),
    )


def _rope_tables(pos):
    half = MLA_ROPE // 2
    inv = 1.0 / (ROPE_THETA ** (jnp.arange(half, dtype=F32) / half))
    ang = pos.astype(F32)[:, None] * inv[None, :]
    cos, sin = jnp.cos(ang), jnp.sin(ang)
    n = pos.shape[0]
    cos_t = jnp.concatenate([cos, cos, jnp.ones((n, 64), F32), jnp.zeros((n, 32), F32)], axis=1)
    sin_t = jnp.concatenate([-sin, sin, jnp.zeros((n, 96), F32)], axis=1)
    return cos_t, sin_t


def _pad_new(a):
    n, s, w = a.shape
    return jnp.concatenate([a, jnp.zeros((n, 8 - s, w), a.dtype)], axis=1)


def _pad_rows(w, rows_per, pad_to):
    g = w.shape[0] // rows_per
    w = w.reshape(g, rows_per, -1)
    return jnp.concatenate([w, jnp.zeros((g, pad_to - rows_per, w.shape[-1]), w.dtype)], axis=1).reshape(
        g * pad_to, -1)


def _sample_pool(u, prefix, w_bd, scale):
    n, L, c = u.shape
    ext = jnp.concatenate([prefix, u], axis=1)
    cs = jnp.concatenate([jnp.zeros((n, 1, c), F32), jnp.cumsum(ext, axis=1)], axis=1)
    hi = cs[:, POOL_PAD + 1:]
    pooled = []
    for g, w in enumerate(POOL_WINDOWS):
        sl = slice(64 * g, 64 * (g + 1))
        lo = cs[:, POOL_PAD + 1 - w:POOL_PAD + 1 - w + L, sl]
        pooled.append((hi[..., sl] - lo) / float(w))
    d = jnp.concatenate(pooled, axis=-1) - u
    y = jnp.dot(d.reshape(n * L, c).astype(BF16), w_bd, preferred_element_type=F32) * scale
    return y, ext[:, -POOL_PAD:]


def _sample_hgrn(zc, s0, lb, onw):
    n, L, _ = zc.shape
    q = zc[..., 0:512].reshape(n, L, HG_HEADS, HG_DK)
    zf = zc[..., 512:1024].reshape(n, L, HG_HEADS, HG_DK)
    v = zc[..., 1024:1280].reshape(n, L, HG_HEADS, HG_DV)
    g = zc[..., 1280:1536].reshape(n, L, HG_HEADS, HG_DV)
    lbh = lb.reshape(HG_HEADS, HG_DK)
    f = lbh + (1.0 - lbh) * jax.nn.sigmoid(zf)
    k = (1.0 - lbh) * jax.nn.sigmoid(-zf)
    s = s0
    outs = []
    for t in range(L):
        s = f[:, t][..., None] * s + k[:, t][..., None] * v[:, t][:, :, None, :]
        outs.append(jnp.sum(s * q[:, t][..., None], axis=2))
    o = jnp.stack(outs, axis=1)
    o = _rms(o, onw, HG_DV) * jax.nn.silu(g)
    return o.reshape(n * L, GROUP_W), s


def _sample_conv(u, prefix, cw, cb, lw, lb, pw, pb):
    n, L, _ = u.shape
    g = u[..., :CONV_C] * jax.nn.sigmoid(u[..., CONV_C:])
    ext = jnp.concatenate([prefix, g], axis=1)
    y = sum(ext[:, j:j + L, :] * cw[j][None, None, :] for j in range(CONV_W)) + cb
    mu = jnp.mean(y, axis=-1, keepdims=True)
    yc = y - mu
    var = jnp.mean(yc * yc, axis=-1, keepdims=True)
    y = yc * lax.rsqrt(var + EPS) * lw + lb
    y = jnp.dot(jax.nn.silu(y).reshape(n * L, CONV_C).astype(BF16), pw, preferred_element_type=F32) + pb
    return y, ext[:, -CONV_PAD:]


def kernel(x_prompt, x_sample, cache_kv_latent, cache_k_rope, cache_mem_k, cache_mem_v, state_pool, state_hgrn, state_conv, page_table, mem_prompt, norm1_w, w_in, pool_w, pool_scale, mla_q_norm_w, mla_w_uq, mla_kv_norm_w, mla_w_ukv, mla_qn_nope_w, mla_qn_rope_w, mla_kn_nope_w, mla_kn_rope_w, hg_lb_param, hg_onorm_w, conv_w, conv_b, conv_ln_w, conv_ln_b, conv_pw_w, conv_pw_b, grp_norm_w, w_out, norm2_w, mem_norm_w, ca_wq, ca_wk, ca_wv, ca_qn_w, ca_kn_w, ca_wo, norm3_w, ffn_w_gate, ffn_w_up, ffn_w_down, moe_router, moe_w_gate, moe_w_up, moe_w_down):
    bp, seq, d = x_prompt.shape
    nb, s_len, _ = x_sample.shape
    depth = w_in.shape[0]
    tp, ts = bp * seq, nb * s_len
    n_past = page_table.shape[1] * PAGE_SIZE
    n_mem = mem_prompt.shape[1]
    first_moe = 1

    sm = jax.nn.softmax(hg_lb_param.astype(F32), axis=0)
    lower_bounds = jnp.cumsum(sm, axis=0) - sm[:1]

    pos = jnp.concatenate([jnp.tile(jnp.arange(seq, dtype=jnp.int32), bp),
                           jnp.tile(n_past + jnp.arange(s_len, dtype=jnp.int32), nb)])
    cos_t, sin_t = _rope_tables(pos)
    cache_rt = jnp.swapaxes(cache_k_rope, 2, 3)
    moe_wg, moe_wu, moe_wd = moe_w_gate.astype(BF16), moe_w_up.astype(BF16), moe_w_down.astype(BF16)
    mem_k_all = cache_mem_k.reshape(depth, nb, n_mem * CA_HEADS, CA_HD)
    mem_v_all = cache_mem_v.reshape(depth, nb, n_mem * CA_HEADS, CA_HD)

    x = jnp.concatenate([x_prompt.reshape(tp, d), x_sample.reshape(ts, d)], axis=0)
    mem2d = mem_prompt.reshape(bp * n_mem, d)
    outs = [[] for _ in range(12)]

    for l in range(depth):
        hp = l <= first_moe
        lw = dict(mla_q_norm_w=mla_q_norm_w[l], mla_w_uq=mla_w_uq[l], mla_kv_norm_w=mla_kv_norm_w[l],
                  mla_w_ukv=mla_w_ukv[l], mla_qn_nope_w=mla_qn_nope_w[l], mla_qn_rope_w=mla_qn_rope_w[l],
                  mla_kn_nope_w=mla_kn_nope_w[l], mla_kn_rope_w=mla_kn_rope_w[l])
        pw = _prep_mla(lw, hp)
        z_a, z_b, z_c, z_d = norm_matmul(x, norm1_w[l], _prep_w_in(w_in[l], hp), W_IN_SPLITS, name="norm_w_in")

        w_bd = _w(jax.scipy.linalg.block_diag(*[pool_w[l, g] for g in range(4)]), hp)
        o_a_p, pool_st = pool_prompt(z_a, w_bd, pool_scale[l], bp, seq)
        o_a_s, pool_st_s = _sample_pool(z_a[tp:].reshape(nb, s_len, GROUP_W), state_pool[l], w_bd[0], pool_scale[l])
        outs[0].append(pool_st[:, 1:])
        outs[7].append(pool_st_s)

        q_p, k_p, v_p, ckv_p, kr_p = mla_proj(z_b, cos_t, sin_t, pw, 0, tp, absorb=False, hp=hp)
        o_b_p = flash_prompt(q_p, k_p, v_p, bp, seq, hp)
        q_s, _, _, ckv_s, kr_s, qabs_s = mla_proj(z_b, cos_t, sin_t, pw, tp, ts, absorb=True, hp=hp)
        rows = MLA_HEADS * s_len
        o_lat = mla_sample(page_table, cache_kv_latent, cache_rt, l,
                           qabs_s.reshape(nb, rows, LANES), q_s.astype(F32).reshape(nb, rows, LANES),
                           _pad_new(ckv_s.reshape(nb, s_len, LANES)), _pad_new(kr_s.reshape(nb, s_len, LANES)),
                           pw["wukt"])
        o_b_s = matmul(o_lat.reshape(ts, 512), pw["wuv_bd"], name="mla_v_up")
        o_b_s = jnp.concatenate([o_b_s.reshape(ts, 4, 64), jnp.zeros((ts, 4, 64), F32)], axis=-1).reshape(ts, 512)
        outs[1].append(ckv_p.reshape(bp, seq, MLA_KV_LORA))
        outs[2].append(kr_p[:, :MLA_ROPE].reshape(bp, seq, MLA_ROPE))
        outs[8].append(ckv_s.reshape(nb, s_len, MLA_KV_LORA))
        outs[9].append(kr_s[:, :MLA_ROPE].reshape(nb, s_len, MLA_ROPE))

        onw_slot = jnp.concatenate([hg_onorm_w[l], jnp.zeros((64,), F32)])
        o_c_p, hg_st = hgrn_prompt(z_c, lower_bounds[l], onw_slot, bp, seq, hp)
        o_c_s, hg_st_s = _sample_hgrn(z_c[tp:].reshape(nb, s_len, -1), state_hgrn[l], lower_bounds[l], hg_onorm_w[l])
        outs[3].append(hg_st[..., :HG_DV])
        outs[10].append(hg_st_s)

        cw_pad = jnp.concatenate([conv_w[l], jnp.zeros((1, CONV_C), F32)], axis=0)
        pw_c = _w(conv_pw_w[l], hp)
        o_d_p, conv_st = conv_prompt(z_d, cw_pad, conv_b[l], conv_ln_w[l], conv_ln_b[l], pw_c, conv_pw_b[l], bp, seq)
        o_d_s, conv_st_s = _sample_conv(z_d[tp:].reshape(nb, s_len, -1), state_conv[l], conv_w[l], conv_b[l],
                                        conv_ln_w[l], conv_ln_b[l], pw_c[0], conv_pw_b[l])
        outs[4].append(conv_st[:, 2:])
        outs[11].append(conv_st_s)

        gnw = grp_norm_w[l]
        gw = [gnw[0:256].reshape(1, -1), _pad_rows(gnw[256:512].reshape(-1, 1), 64, 128).reshape(1, -1),
              gnw[512:768].reshape(1, -1), gnw[768:1024].reshape(1, -1)]
        wo_l = w_out[l]
        w_out_parts = [_w(wo_l[0:256], hp), _w(_pad_rows(wo_l[256:512], 64, 128), hp),
                       _w(wo_l[512:768], hp), _w(wo_l[768:1024], hp)]
        x1, q_ca = mix_out((o_a_p, o_b_p, o_c_p, o_d_p), (o_a_s, o_b_s, o_c_s, o_d_s), x, gw, w_out_parts,
                           norm2_w[l], _w(ca_wq[l], hp))

        w_kv = _w(jnp.concatenate([ca_wk[l], ca_wv[l]], axis=1), hp)
        mk_p, mv_p = norm_matmul(mem2d, mem_norm_w[l], w_kv, (CA_W, CA_W), slot_norm_w=ca_kn_w[l],
                                 tile=min(TOK_TILE, mem2d.shape[0]), name="memory_kv")
        outs[5].append(mk_p.reshape(bp, n_mem, CA_HEADS, CA_HD))
        outs[6].append(mv_p.reshape(bp, n_mem, CA_HEADS, CA_HD))

        wo_ca = _w(ca_wo[l], hp)
        x2_p = cross_attend(q_ca, mk_p.reshape(bp, n_mem, CA_W), mv_p.reshape(bp, n_mem, CA_W), x1,
                            ca_qn_w[l], wo_ca, bp, seq, tl=min(512, seq))
        x2_s = cross_attend_sample(q_ca[tp:], mem_k_all, mem_v_all, x1[tp:], ca_qn_w[l], wo_ca, s_len, l)
        x2 = jnp.concatenate([x2_p, x2_s], axis=0)

        j = l // 2
        if l % 2 == 0:
            x = ffn_dense(x2, norm3_w[l], _w(ffn_w_gate[j], hp), _w(ffn_w_up[j], hp), _w(ffn_w_down[j], hp),
                          ff_block=FF_CHUNK if hp else D_FF)
        else:
            router_pad = jnp.concatenate([moe_router[j], jnp.zeros((d, LANES - N_EXPERTS), F32)], axis=1)
            x = moe_ffn(x2, norm3_w[l], router_pad, moe_wg, moe_wu, moe_wd, j)

    st = lambda k: jnp.stack(outs[k])
    return (x[:tp].reshape(bp, seq, d), x[tp:].reshape(nb, s_len, d),
            st(0), st(1), st(2), st(3), st(4), st(5), st(6), st(7), st(8), st(9), st(10), st(11))
```
